```python
import math, functools
import jax, jax.numpy as jnp
from jax import lax
import numpy as np

D_MODEL = 1024
BATCH = 2
SEQ = 8192
DEPTH = 2
DEC_BATCH = 32
DEC_SEQ = 8
PAST_LEN = 16384
PAGE_SIZE = 128

HEAD_DIM = 64
MIX_WIDTH = D_MODEL
W_A = MIX_WIDTH // 2
W_B = MIX_WIDTH // 4
W_C = MIX_WIDTH - W_A - W_B
H_A = W_A // HEAD_DIM
H_B = W_B // HEAD_DIM
DK_B = HEAD_DIM // 2
DV_B = HEAD_DIM
H_C = W_C // HEAD_DIM
DK_C = HEAD_DIM
DV_C = HEAD_DIM
GLA_RANK = 16
GLA_TAU = 16.0
GLA_CHUNK = 64
RET_CHUNK = 64
Q_BLOCK = 128
D_FF = 4 * D_MODEL
D_PLE = 256
ROPE_BASE = 10000.0
NORM_EPS = 1e-6
SPLIT_SIZES = (W_A, W_A, W_A, H_A,
               H_B * DK_B, H_B * DK_B, W_B, GLA_RANK, W_B,
               H_C * DK_C, H_C * DK_C, W_C, W_C)
SPLIT_IDX = tuple(int(i) for i in np.cumsum(SPLIT_SIZES)[:-1])
D_IN = sum(SPLIT_SIZES)

kernel_name = "fox_gla_retention_hybrid_step"


def rms_norm(x, g):
    xf = x.astype(jnp.float32)
    y = xf * lax.rsqrt(jnp.mean(xf * xf, axis=-1, keepdims=True) + NORM_EPS)
    return (y * g.astype(jnp.float32)).astype(x.dtype)


def head_layer_norm(x, g):
    xf = x.astype(jnp.float32)
    mu = jnp.mean(xf, axis=-1, keepdims=True)
    var = jnp.mean(jnp.square(xf - mu), axis=-1, keepdims=True)
    return ((xf - mu) * lax.rsqrt(var + NORM_EPS) * g.astype(jnp.float32)).astype(x.dtype)


def rotary(x, pos):
    half = x.shape[-1] // 2
    freqs = ROPE_BASE ** (-jnp.arange(half, dtype=jnp.float32) / half)
    ang = pos[:, None] * freqs[None, :]
    cos = jnp.cos(ang)[None, :, None, :]
    sin = jnp.sin(ang)[None, :, None, :]
    xf = x.astype(jnp.float32)
    x1, x2 = xf[..., :half], xf[..., half:]
    return jnp.concatenate([x1 * cos - x2 * sin, x1 * sin + x2 * cos], axis=-1).astype(x.dtype)


def retention_log_decay():
    return jnp.log1p(-jnp.exp2(-5.0 - jnp.arange(H_C, dtype=jnp.float32)))


def chunked_gated_linear_attention(q, k, v, log_a, s0, chunk):
    B, T, H, K = q.shape
    V = v.shape[-1]
    C = math.gcd(T, chunk)
    N = T // C
    def blocks(a):
        return jnp.swapaxes(a.astype(jnp.float32).reshape(B, N, C, *a.shape[2:]), 0, 1)
    tri = jnp.tril(jnp.ones((C, C), dtype=bool))

    def step(S, inp):
        qc, kc, vc, ac = inp
        b = jnp.cumsum(ac, axis=1)
        o_inter = jnp.einsum('bthk,bhkv->bthv', qc * jnp.exp(b), S)
        diff = b[:, :, None] - b[:, None, :]
        decay = jnp.exp(jnp.where(tri[None, :, :, None, None], diff, -jnp.inf))
        att = jnp.einsum('bthk,bshk,btshk->bhts', qc, kc, decay)
        o_intra = jnp.einsum('bhts,bshv->bthv', att, vc)
        b_last = b[:, -1]
        S_new = jnp.exp(b_last)[..., None] * S + jnp.einsum(
            'bshk,bshv->bhkv', kc * jnp.exp(b_last[:, None] - b), vc)
        return S_new, o_inter + o_intra

    S, o = lax.scan(step, s0.astype(jnp.float32), (blocks(q), blocks(k), blocks(v), blocks(log_a)))
    o = jnp.swapaxes(o, 0, 1).reshape(B, T, H, V)
    return o.astype(v.dtype), S.astype(s0.dtype)


def fox_prompt(q, k, v, logf):
    B, S, H, D = q.shape
    qb_len = math.gcd(S, Q_BLOCK)
    nb = S // qb_len
    c_t = jnp.cumsum(logf, axis=1).transpose(0, 2, 1)
    q_blocks = jnp.swapaxes(q.reshape(B, nb, qb_len, H, D), 0, 1)
    c_blocks = jnp.moveaxis(c_t.reshape(B, H, nb, qb_len), 2, 0)
    kpos = jnp.arange(S)
    scale = HEAD_DIM ** -0.5

    def block(args):
        qi, cq, i = args
        s = jnp.einsum('bqhd,bkhd->bhqk', qi, k).astype(jnp.float32) * scale
        s = s + cq[..., :, None] - c_t[:, :, None, :]
        qpos = i * qb_len + jnp.arange(qb_len)
        s = jnp.where(kpos[None, :] <= qpos[:, None], s, -jnp.inf)
        p = jax.nn.softmax(s, axis=-1)
        return jnp.einsum('bhqk,bkhd->bqhd', p.astype(v.dtype), v)

    o = lax.map(block, (q_blocks, c_blocks, jnp.arange(nb)))
    return jnp.swapaxes(o, 0, 1).reshape(B, S, H, D)


def fox_sample(q, k, v, logf, k_past, v_past, logf_past):
    T = q.shape[1]
    P = k_past.shape[1]
    scale = HEAD_DIM ** -0.5
    c_past = jnp.cumsum(logf_past.astype(jnp.float32), axis=1)
    c_new = c_past[:, -1:] + jnp.cumsum(logf.astype(jnp.float32), axis=1)
    cn = c_new.transpose(0, 2, 1)
    cp = c_past.transpose(0, 2, 1)
    s_past = jnp.einsum('bthd,bshd->bhts', q, k_past).astype(jnp.float32) * scale
    s_past = s_past + cn[..., :, None] - cp[:, :, None, :]
    s_new = jnp.einsum('bthd,bshd->bhts', q, k).astype(jnp.float32) * scale
    s_new = s_new + cn[..., :, None] - cn[:, :, None, :]
    tri = jnp.tril(jnp.ones((T, T), dtype=bool))
    s_new = jnp.where(tri, s_new, -jnp.inf)
    p = jax.nn.softmax(jnp.concatenate([s_past, s_new], axis=-1), axis=-1)
    return (jnp.einsum('bhts,bshd->bthd', p[..., :P].astype(v.dtype), v_past)
            + jnp.einsum('bhts,bshd->bthd', p[..., P:].astype(v.dtype), v))


def decoder_layer(h, p, pos, fox_attend, s_gla0, s_ret0,
                  norm1, w_in, b_f, fox_q_norm, fox_k_norm, gla_w_a2, gla_b_a,
                  gla_o_norm, ret_o_norm, w_out, norm2, w_up, w_down, ple_norm, w_pg, w_pe):
    B, T, _ = h.shape
    xn = rms_norm(h, norm1)
    z = xn @ w_in
    (qa, ka, va, fa, qb, kb, vb, ab, gb, qc, kc, vc, gc) = jnp.split(z, SPLIT_IDX, axis=-1)

    qa = rms_norm(qa.reshape(B, T, H_A, HEAD_DIM), fox_q_norm)
    ka = rms_norm(ka.reshape(B, T, H_A, HEAD_DIM), fox_k_norm)
    va = va.reshape(B, T, H_A, HEAD_DIM)
    logf = jax.nn.log_sigmoid((fa + b_f).astype(jnp.float32))
    oa = fox_attend(qa, ka, va, logf).reshape(B, T, W_A)

    qb = qb.reshape(B, T, H_B, DK_B) * (DK_B ** -0.5)
    kb = kb.reshape(B, T, H_B, DK_B)
    vb = vb.reshape(B, T, H_B, DV_B)
    log_alpha = (jax.nn.log_sigmoid((ab @ gla_w_a2 + gla_b_a).astype(jnp.float32))
                 / GLA_TAU).reshape(B, T, H_B, DK_B)
    ob, s_gla = chunked_gated_linear_attention(qb, kb, vb, log_alpha, s_gla0, GLA_CHUNK)
    ob = rms_norm(ob, gla_o_norm.reshape(H_B, DV_B)).reshape(B, T, W_B) * jax.nn.silu(gb)

    qc = rotary(qc.reshape(B, T, H_C, DK_C), pos)
    kc = rotary(kc.reshape(B, T, H_C, DK_C), pos) * (DK_C ** -0.5)
    vc = vc.reshape(B, T, H_C, DV_C)
    log_gamma = jnp.broadcast_to(retention_log_decay()[:, None], (B, T, H_C, DK_C))
    oc, s_ret = chunked_gated_linear_attention(qc, kc, vc, log_gamma, s_ret0, RET_CHUNK)
    oc = head_layer_norm(oc, ret_o_norm.reshape(H_C, DV_C)).reshape(B, T, W_C) * jax.nn.silu(gc)

    h = h + jnp.concatenate([oa, ob, oc], axis=-1) @ w_out
    u = rms_norm(h, norm2) @ w_up
    h = h + jnp.square(jax.nn.relu(u)) @ w_down
    h = h + jax.nn.sigmoid(rms_norm(h, ple_norm) @ w_pg) * (p @ w_pe)
    return h, ka, va, logf, s_gla, s_ret


def setup_inputs(seed: int = 0) -> dict:
    key = jax.random.key(seed)
    ks = iter(jax.random.split(key, 40))
    def nrm(shape, scale=1.0):
        return scale * jax.random.normal(next(ks), shape, jnp.float32)
    n_pages = PAST_LEN // PAGE_SIZE
    n_used = DEC_BATCH * n_pages
    n_pool = n_used + n_used // 4
    page_table = jax.random.permutation(next(ks), n_pool)[:n_used].reshape(DEC_BATCH, n_pages).astype(jnp.int32)
    return {
        "x_prompt": nrm((BATCH, SEQ, D_MODEL)),
        "x_sample": nrm((DEC_BATCH, DEC_SEQ, D_MODEL)),
        "cache_k": nrm((DEPTH, n_pool, PAGE_SIZE, H_A, HEAD_DIM)),
        "cache_v": nrm((DEPTH, n_pool, PAGE_SIZE, H_A, HEAD_DIM)),
        "cache_logf": jax.nn.log_sigmoid(3.0 + nrm((DEPTH, n_pool, PAGE_SIZE, H_A), 0.5)),
        "state_gla": nrm((DEPTH, DEC_BATCH, H_B, DK_B, DV_B), 0.5),
        "state_ret": nrm((DEPTH, DEC_BATCH, H_C, DK_C, DV_C), 0.5),
        "page_table": page_table,
        "p_prompt": nrm((DEPTH, BATCH, SEQ, D_PLE)),
        "p_sample": nrm((DEPTH, DEC_BATCH, DEC_SEQ, D_PLE)),
        "norm1": 1.0 + nrm((DEPTH, D_MODEL), 0.05),
        "w_in": nrm((DEPTH, D_MODEL, D_IN), D_MODEL ** -0.5),
        "b_f": 2.0 + nrm((DEPTH, H_A), 0.5),
        "fox_q_norm": 1.0 + nrm((DEPTH, HEAD_DIM), 0.05),
        "fox_k_norm": 1.0 + nrm((DEPTH, HEAD_DIM), 0.05),
        "gla_w_a2": nrm((DEPTH, GLA_RANK, H_B * DK_B), GLA_RANK ** -0.5),
        "gla_b_a": nrm((DEPTH, H_B * DK_B), 0.1),
        "gla_o_norm": 1.0 + nrm((DEPTH, W_B), 0.05),
        "ret_o_norm": 1.0 + nrm((DEPTH, W_C), 0.05),
        "w_out": nrm((DEPTH, MIX_WIDTH, D_MODEL), MIX_WIDTH ** -0.5),
        "norm2": 1.0 + nrm((DEPTH, D_MODEL), 0.05),
        "w_up": nrm((DEPTH, D_MODEL, D_FF), D_MODEL ** -0.5),
        "w_down": nrm((DEPTH, D_FF, D_MODEL), D_FF ** -0.5),
        "ple_norm": 1.0 + nrm((DEPTH, D_MODEL), 0.05),
        "w_pg": nrm((DEPTH, D_MODEL, D_MODEL), D_MODEL ** -0.5),
        "w_pe": nrm((DEPTH, D_PLE, D_MODEL), D_PLE ** -0.5),
    }


def reference(x_prompt, x_sample, cache_k, cache_v, cache_logf, state_gla, state_ret, page_table,
              p_prompt, p_sample, norm1, w_in, b_f, fox_q_norm, fox_k_norm, gla_w_a2, gla_b_a,
              gla_o_norm, ret_o_norm, w_out, norm2, w_up, w_down, ple_norm, w_pg, w_pe):
    bp = x_prompt.shape[0]
    db = x_sample.shape[0]
    past = page_table.shape[1] * PAGE_SIZE
    pos_p = jnp.arange(x_prompt.shape[1], dtype=jnp.float32)
    pos_s = past + jnp.arange(x_sample.shape[1], dtype=jnp.float32)
    hp, hs = x_prompt, x_sample
    new_p, new_s = [], []
    for l in range(DEPTH):
        lw = (norm1[l], w_in[l], b_f[l], fox_q_norm[l], fox_k_norm[l], gla_w_a2[l], gla_b_a[l],
              gla_o_norm[l], ret_o_norm[l], w_out[l], norm2[l], w_up[l], w_down[l],
              ple_norm[l], w_pg[l], w_pe[l])
        zg = jnp.zeros((bp, H_B, DK_B, DV_B), x_prompt.dtype)
        zr = jnp.zeros((bp, H_C, DK_C, DV_C), x_prompt.dtype)
        hp, kp_l, vp_l, lfp_l, sgp_l, srp_l = decoder_layer(hp, p_prompt[l], pos_p, fox_prompt, zg, zr, *lw)
        new_p.append((kp_l, vp_l, lfp_l, sgp_l, srp_l))
        k_past = cache_k[l][page_table].reshape(db, past, H_A, HEAD_DIM)
        v_past = cache_v[l][page_table].reshape(db, past, H_A, HEAD_DIM)
        lf_past = cache_logf[l][page_table].reshape(db, past, H_A)
        attend = functools.partial(fox_sample, k_past=k_past, v_past=v_past, logf_past=lf_past)
        hs, ks_l, vs_l, lfs_l, sgs_l, srs_l = decoder_layer(hs, p_sample[l], pos_s, attend,
                                                            state_gla[l], state_ret[l], *lw)
        new_s.append((ks_l, vs_l, lfs_l, sgs_l, srs_l))
    k_p, v_p, lf_p, gla_p, ret_p = [jnp.stack(a) for a in zip(*new_p)]
    k_s, v_s, lf_s, gla_s, ret_s = [jnp.stack(a) for a in zip(*new_s)]
    return (hp, hs, k_p, v_p, lf_p, gla_p, ret_p, k_s, v_s, lf_s, gla_s, ret_s)
```

```python
import functools

import numpy as np
import jax
import jax.numpy as jnp
from jax import lax
from jax.experimental import pallas as pl
from jax.experimental.pallas import tpu as pltpu

F32 = jnp.float32
BF16 = jnp.bfloat16

D_MODEL = 1024
HEAD_DIM = 64
H_A = 8
W_A = H_A * HEAD_DIM
H_B = 4
DK_B = 32
DV_B = 64
W_B = H_B * DV_B
H_C = 4
DK_C = 64
DV_C = 64
W_C = H_C * DV_C
GLA_RANK = 16
GLA_TAU = 16.0
GLA_CHUNK = 64
RET_CHUNK = 64
D_FF = 4 * D_MODEL
D_PLE = 256
PAGE_SIZE = 128
ROPE_BASE = 10000.0
NORM_EPS = 1e-6
SPLIT_SIZES = (W_A, W_A, W_A, H_A,
               H_B * DK_B, H_B * DK_B, W_B, GLA_RANK, W_B,
               H_C * DK_C, H_C * DK_C, W_C, W_C)
SPLIT_IDX = tuple(int(i) for i in np.cumsum(SPLIT_SIZES)[:-1])

LANES = 128
NEG_BIG = -1e30
VMEM_LIMIT = 56 * 1024 * 1024

_OFF_QA, _OFF_KA, _OFF_VA, _OFF_FA = 0, 512, 1024, 1536
_OFF_QB, _OFF_KB, _OFF_VB, _OFF_AB, _OFF_GB = 1664, 1792, 1920, 2176, 2304
_OFF_QC, _OFF_KC, _OFF_VC, _OFF_GC = 2560, 2816, 3072, 3328
D_IN_PAD = 3584

ROW_TILE = 512
ATT_TILE = 512
PAGES_PER_STEP = 8


def _dot(a, b):
    return jnp.dot(a, b, preferred_element_type=F32)


def _dot_nt(a, b):
    return lax.dot_general(a, b, (((1,), (1,)), ((), ())), preferred_element_type=F32)


def _dot_tn(a, b):
    return lax.dot_general(a, b, (((0,), (0,)), ((), ())), preferred_element_type=F32)


def _split3(x):
    hi = x.astype(BF16)
    r = x - hi.astype(F32)
    mid = r.astype(BF16)
    lo = (r - mid.astype(F32)).astype(BF16)
    return hi, mid, lo


def _dot3_left(w, x):
    hi, mid, lo = _split3(x)
    return _dot(w, hi) + _dot(w, mid) + _dot(w, lo)


def _dot3_right(x, w):
    hi, mid, lo = _split3(x)
    return _dot(hi, w) + _dot(mid, w) + _dot(lo, w)


def _log_sigmoid(x):
    return jnp.minimum(x, 0.0) - jnp.log1p(jnp.exp(-jnp.abs(x)))


def _sigmoid(x):
    return 1.0 / (1.0 + jnp.exp(-x))


def _params(sem):
    return pltpu.CompilerParams(dimension_semantics=sem, vmem_limit_bytes=VMEM_LIMIT)


def _const_spec(shape):
    nd = len(shape)
    return pl.BlockSpec(shape, lambda *_: (0,) * nd)


def _inproj_kernel(x_ref, g1_ref, wp_ref, bf_ref, gq_ref, gk_ref, g512_ref, wa2_ref, ba_ref,
                   cos_ref, sin_ref, tri_ref,
                   qa_ref, k32_ref, v32_ref, kbf_ref, vbf_ref, logf_ref, c_ref,
                   qb_ref, kb_ref, vb_ref, la_ref, gb_ref,
                   qc_ref, kc_ref, vc_ref, gc_ref,
                   carry_ref, *, blocks_per_seq):
    i = pl.program_id(0)
    x = x_ref[...]
    ms = jnp.mean(x * x, axis=-1, keepdims=True)
    xn = (x * lax.rsqrt(ms + NORM_EPS) * g1_ref[...]).astype(BF16)

    za = _dot(xn, wp_ref[:, _OFF_QA:_OFF_QB])
    g512 = g512_ref[...]

    def head_rms(t, gain):
        msq = _dot((t * t).astype(BF16), g512)
        return t * lax.rsqrt(msq + NORM_EPS) * gain

    qa_ref[...] = head_rms(za[:, 0:512], gq_ref[...]).astype(BF16)
    k = head_rms(za[:, 512:1024], gk_ref[...])
    k32_ref[...] = k
    kbf_ref[...] = k.astype(BF16)
    v = za[:, 1024:1536]
    v32_ref[...] = v
    vbf_ref[...] = v.astype(BF16)
    logf = _log_sigmoid(za[:, 1536:1664] + bf_ref[...])
    logf_ref[...] = logf

    @pl.when(i % blocks_per_seq == 0)
    def _():
        carry_ref[...] = jnp.zeros_like(carry_ref)

    c = _dot3_left(tri_ref[...], logf) + carry_ref[...]
    c_ref[...] = c
    carry_ref[...] = c[c.shape[0] - 1:, :]

    zb = _dot(xn, wp_ref[:, _OFF_QB:_OFF_QC])
    qb_ref[...] = zb[:, 0:128]
    kb_ref[...] = zb[:, 128:256]
    vb_ref[...] = zb[:, 256:512].astype(BF16)
    ab = zb[:, 512:640].astype(BF16)
    la_ref[...] = _log_sigmoid(_dot(ab, wa2_ref[...]) + ba_ref[...]) * (1.0 / GLA_TAU)
    gb_ref[...] = zb[:, 640:896]

    zc = _dot(xn, wp_ref[:, _OFF_QC:D_IN_PAD])
    cos = cos_ref[...]
    sin = sin_ref[...]

    def rot(t):
        t1, t2 = t[:, 0:128], t[:, 128:256]
        return jnp.concatenate([t1 * cos - t2 * sin, t1 * sin + t2 * cos], axis=1)

    qc_ref[...] = rot(zc[:, 0:256])
    kc_ref[...] = rot(zc[:, 256:512]) * (DK_C ** -0.5)
    vc_ref[...] = zc[:, 512:768].astype(BF16)
    gc_ref[...] = zc[:, 768:1024]


def _inproj(x2d, lw, cos_tab, sin_tab, tm, blocks_per_seq):
    n = x2d.shape[0]
    nb_tab = cos_tab.shape[0] // tm
    row = lambda w: pl.BlockSpec((tm, w), lambda i: (i, 0))
    tab = pl.BlockSpec((tm, LANES), lambda i: (i % nb_tab, 0))
    tri = jnp.tril(jnp.ones((tm, tm), F32)).astype(BF16)
    widths = (512, 512, 512, 512, 512, 128, 128, 128, 128, 256, 128, 256, 256, 256, 256, 256)
    dtypes = (BF16, F32, F32, BF16, BF16, F32, F32, F32, F32, BF16, F32, F32, F32, F32, BF16, F32)
    consts = (lw["g1"], lw["wp"], lw["bf"], lw["gq"], lw["gk"], lw["g512"], lw["wa2"], lw["ba"])
    return pl.pallas_call(
        functools.partial(_inproj_kernel, blocks_per_seq=blocks_per_seq),
        grid=(n // tm,),
        in_specs=[row(D_MODEL)] + [_const_spec(a.shape) for a in consts] + [tab, tab, _const_spec(tri.shape)],
        out_specs=[row(w) for w in widths],
        out_shape=[jax.ShapeDtypeStruct((n, w), d) for w, d in zip(widths, dtypes)],
        scratch_shapes=[pltpu.VMEM((1, LANES), F32)],
        compiler_params=_params(("arbitrary",)),
        name="inproj",
    )(x2d, *consts, cos_tab, sin_tab, tri)


def _flash_kernel(qi_ref, ki_ref, q_ref, k_ref, v_ref, cq_ref, ck_ref, o_ref, m_ref, l_ref, acc_ref):
    step = pl.program_id(2)
    qi = qi_ref[step]
    ki = ki_ref[step]
    tq = q_ref.shape[1]
    tk = k_ref.shape[1]

    @pl.when(ki == 0)
    def _():
        m_ref[...] = jnp.full_like(m_ref, NEG_BIG)
        l_ref[...] = jnp.zeros_like(l_ref)
        acc_ref[...] = jnp.zeros_like(acc_ref)

    def body(diagonal):
        q = q_ref[0]
        k = k_ref[0]
        v = v_ref[0]
        lane = lax.broadcasted_iota(jnp.int32, (1, LANES), 1)
        if diagonal:
            keep = (lax.broadcasted_iota(jnp.int32, (tq, tk), 1)
                    <= lax.broadcasted_iota(jnp.int32, (tq, tk), 0))
        for hh in range(2):
            in_head = (lane < HEAD_DIM) if hh == 0 else (lane >= HEAD_DIM)
            qm = jnp.where(in_head, q, jnp.zeros_like(q))
            s = _dot_nt(qm, k)
            c0 = cq_ref[0, hh, :, 0:1]
            s = s - (ck_ref[0, hh] - c0)
            if diagonal:
                s = jnp.where(keep, s, NEG_BIG)
            m_prev = m_ref[hh]
            m_new = jnp.maximum(m_prev, jnp.max(s, axis=1, keepdims=True))
            alpha = jnp.exp(m_prev - m_new)
            p = jnp.exp(s - m_new)
            l_ref[hh] = alpha * l_ref[hh] + jnp.sum(p, axis=1, keepdims=True)
            acc_ref[hh] = alpha * acc_ref[hh] + _dot(p.astype(BF16), v)
            m_ref[hh] = m_new

    @pl.when(ki < qi)
    def _():
        body(False)

    @pl.when(ki == qi)
    def _():
        body(True)
        lane = lax.broadcasted_iota(jnp.int32, (1, LANES), 1)
        o0 = acc_ref[0] / l_ref[0]
        o1 = acc_ref[1] / l_ref[1]
        o_ref[0] = jnp.where(lane < HEAD_DIM, o0, o1).astype(o_ref.dtype)


def _flash(q, k, v, c_t, t):
    b, s, _ = q.shape
    nq = s // t
    qi_np = np.array([i for i in range(nq) for _ in range(i + 1)], np.int32)
    ki_np = np.array([j for i in range(nq) for j in range(i + 1)], np.int32)
    grid_spec = pltpu.PrefetchScalarGridSpec(
        num_scalar_prefetch=2,
        grid=(b, H_A // 2, len(qi_np)),
        in_specs=[
            pl.BlockSpec((1, t, LANES), lambda bb, hp, st, qi, ki: (bb, qi[st], hp)),
            pl.BlockSpec((1, t, LANES), lambda bb, hp, st, qi, ki: (bb, ki[st], hp)),
            pl.BlockSpec((1, t, LANES), lambda bb, hp, st, qi, ki: (bb, ki[st], hp)),
            pl.BlockSpec((1, 2, 1, t), lambda bb, hp, st, qi, ki: (bb, hp, 0, qi[st])),
            pl.BlockSpec((1, 2, 1, t), lambda bb, hp, st, qi, ki: (bb, hp, 0, ki[st])),
        ],
        out_specs=pl.BlockSpec((1, t, LANES), lambda bb, hp, st, qi, ki: (bb, qi[st], hp)),
        scratch_shapes=[pltpu.VMEM((2, t, 1), F32), pltpu.VMEM((2, t, 1), F32),
                        pltpu.VMEM((2, t, LANES), F32)],
    )
    return pl.pallas_call(
        _flash_kernel,
        grid_spec=grid_spec,
        out_shape=jax.ShapeDtypeStruct((b, s, W_A), BF16),
        compiler_params=_params(("arbitrary", "arbitrary", "arbitrary")),
        name="fox_prompt",
    )(jnp.asarray(qi_np), jnp.asarray(ki_np), q, k, v, c_t, c_t)


def _decode_kernel(pt_ref, qm_ref, kn_ref, vn_ref, lfn_ref, w_ref, *rest, n_pages):
    g = n_pages
    k_refs = rest[0:g]
    v_refs = rest[g:2 * g]
    lf_refs = rest[2 * g:3 * g]
    o_ref, m_ref, l_ref, acc_ref, carry_ref = rest[3 * g:]
    j = pl.program_id(1)
    qm = qm_ref[0]
    rows = qm.shape[0]

    def attend(bias, kt, vt):
        s = _dot(qm, kt) + bias
        m_prev = m_ref[...]
        m_new = jnp.maximum(m_prev, jnp.max(s, axis=1, keepdims=True))
        alpha = jnp.exp(m_prev - m_new)
        p = jnp.exp(s - m_new)
        l_ref[...] = alpha * l_ref[...] + jnp.sum(p, axis=1, keepdims=True)
        acc_ref[...] = alpha * acc_ref[...] + _dot_nt(p.astype(BF16), vt)
        m_ref[...] = m_new

    w = w_ref[...]

    @pl.when(j == 0)
    def _():
        m_ref[...] = jnp.full_like(m_ref, NEG_BIG)
        l_ref[...] = jnp.zeros_like(l_ref)
        acc_ref[...] = jnp.zeros_like(acc_ref)
        carry_ref[...] = jnp.zeros_like(carry_ref)
        y = _dot3_right(lfn_ref[0], w)
        cum = y[:, LANES:] - y[:, :LANES]
        bias = jnp.concatenate([-cum] * (rows // H_A), axis=0)
        tok = lax.broadcasted_iota(jnp.int32, (rows, LANES), 0) // H_A
        col = lax.broadcasted_iota(jnp.int32, (rows, LANES), 1)
        attend(jnp.where(col <= tok, bias, NEG_BIG), kn_ref[0], vn_ref[0])

    x = jnp.concatenate([r[...] for r in lf_refs], axis=0)
    y = _dot3_right(x, w)
    off = carry_ref[...]
    pieces = []
    for i in range(g):
        pieces.append(y[H_A * i:H_A * (i + 1), :LANES] + off)
        off = off + y[H_A * i:H_A * (i + 1), LANES:]
    carry_ref[...] = off
    bias_t = jnp.concatenate(pieces, axis=1)
    bias = jnp.concatenate([bias_t] * (rows // H_A), axis=0)
    page_t = lambda r: r[...].reshape(W_A, PAGE_SIZE).astype(BF16)
    kt = jnp.concatenate([page_t(r) for r in k_refs], axis=1)
    vt = jnp.concatenate([page_t(r) for r in v_refs], axis=1)
    attend(bias, kt, vt)

    @pl.when(j == pl.num_programs(1) - 1)
    def _():
        o = acc_ref[...] / l_ref[...]
        head_row = lax.broadcasted_iota(jnp.int32, o.shape, 0) % H_A
        head_col = lax.broadcasted_iota(jnp.int32, o.shape, 1) // HEAD_DIM
        o = jnp.where(head_row == head_col, o, 0.0)
        t = rows // H_A
        o_ref[0] = jnp.sum(o.reshape(t, H_A, o.shape[1]), axis=1).astype(o_ref.dtype)


def _decode(layer, qm, k_new, v_new, lf_new_t, cache_k, cache_v, cache_lf_t, page_table, g):
    db, n_pages = page_table.shape
    t = qm.shape[1] // H_A
    steps = n_pages // g
    suffix = (np.arange(LANES)[:, None] > np.arange(LANES)[None, :]).astype(np.float32)
    w = jnp.asarray(np.concatenate([suffix, np.ones((LANES, LANES), np.float32)], axis=1), BF16)

    def page_map(i, trailing):
        def index_map(b, j, pt):
            return (layer, pt[b, n_pages - 1 - (j * g + i)]) + (0,) * trailing
        return index_map

    seq = lambda shape: pl.BlockSpec((1,) + shape, lambda b, j, pt: (b, 0, 0))
    in_specs = [seq(qm.shape[1:]), seq((W_A, LANES)), seq((W_A, LANES)), seq((H_A, LANES)),
                pl.BlockSpec(w.shape, lambda b, j, pt: (0, 0))]
    kv_page = (None, None, H_A, HEAD_DIM, PAGE_SIZE)
    in_specs += [pl.BlockSpec(kv_page, page_map(i, 3)) for i in range(g)]
    in_specs += [pl.BlockSpec(kv_page, page_map(i, 3)) for i in range(g)]
    in_specs += [pl.BlockSpec((None, None, H_A, PAGE_SIZE), page_map(i, 2)) for i in range(g)]
    rows = qm.shape[1]
    grid_spec = pltpu.PrefetchScalarGridSpec(
        num_scalar_prefetch=1,
        grid=(db, steps),
        in_specs=in_specs,
        out_specs=pl.BlockSpec((1, t, W_A), lambda b, j, pt: (b, 0, 0)),
        scratch_shapes=[pltpu.VMEM((rows, 1), F32), pltpu.VMEM((rows, 1), F32),
                        pltpu.VMEM((rows, W_A), F32), pltpu.VMEM((H_A, LANES), F32)],
    )
    return pl.pallas_call(
        functools.partial(_decode_kernel, n_pages=g),
        grid_spec=grid_spec,
        out_shape=jax.ShapeDtypeStruct((db, t, W_A), BF16),
        compiler_params=_params(("arbitrary", "arbitrary")),
        name="fox_sample",
    )(page_table, qm, k_new, v_new, lf_new_t, w,
      *([cache_k] * g), *([cache_v] * g), *([cache_lf_t] * g))


def _recurrent_kernel(qb_ref, kb_ref, vb_ref, la_ref, gb_ref, qc_ref, kc_ref, vc_ref, gc_ref,
                      sg0_ref, sr0_ref, tri_ref, g256_ref, gno_ref, rno_ref,
                      dq_ref, kdec_ref, dmat_ref, sdec_ref, mg_ref, mr_ref,
                      ob_ref, oc_ref, sg_ref, sr_ref, stg_ref, str_ref):
    j = pl.program_id(1)
    c = qb_ref.shape[0]

    @pl.when(j == 0)
    def _():
        stg_ref[...] = sg0_ref[0]
        str_ref[...] = sr0_ref[0]

    tri = tri_ref[...]
    causal = (lax.broadcasted_iota(jnp.int32, (c, c), 1) <= lax.broadcasted_iota(jnp.int32, (c, c), 0))
    lane_k = lax.broadcasted_iota(jnp.int32, (1, LANES), 1)
    lane_v = lax.broadcasted_iota(jnp.int32, (1, 2 * LANES), 1)
    g256 = g256_ref[...]

    la = la_ref[...]
    hi = la.astype(BF16)
    lo = (la - hi.astype(F32)).astype(BF16)
    b = _dot(tri, hi) + _dot(tri, lo)
    b_last = b[c - 1:, :]
    qs = (qb_ref[...] * (jnp.exp(b) * (DK_B ** -0.5))).astype(BF16)
    kinv = (kb_ref[...] * jnp.exp(-b)).astype(BF16)
    kend = (kb_ref[...] * jnp.exp(b_last - b)).astype(BF16)
    vb = vb_ref[...]
    st = stg_ref[...]
    o = _dot_nt(qs, st.astype(BF16))
    for h in range(H_B):
        qh = jnp.where(lane_k // DK_B == h, qs, jnp.zeros_like(qs))
        att = jnp.where(causal, _dot_nt(qh, kinv), 0.0)
        vh = jnp.where(lane_v // DV_B == h, vb, jnp.zeros_like(vb))
        o = o + _dot(att.astype(BF16), vh)
    stg_ref[...] = st * jnp.exp(b_last) + _dot_tn(vb, kend) * mg_ref[...]
    msq = _dot((o * o).astype(BF16), g256)
    gate = gb_ref[...]
    ob_ref[...] = (o * lax.rsqrt(msq + NORM_EPS) * gno_ref[...] * (gate * _sigmoid(gate))).astype(ob_ref.dtype)

    q = qc_ref[...].astype(BF16)
    kc = kc_ref[...]
    vc = vc_ref[...]
    st = str_ref[...]
    o = _dot_nt(q, st.astype(BF16)) * dq_ref[...]
    kbf = kc.astype(BF16)
    for h in range(H_C):
        qh = jnp.where((lane_v % LANES) // (DK_C // 2) == h, q, jnp.zeros_like(q))
        att = _dot_nt(qh, kbf) * dmat_ref[h]
        vh = jnp.where(lane_v // DV_C == h, vc, jnp.zeros_like(vc))
        o = o + _dot(att.astype(BF16), vh)
    str_ref[...] = st * sdec_ref[...] + _dot_tn(vc, (kc * kdec_ref[...]).astype(BF16)) * mr_ref[...]
    ohi = o.astype(BF16)
    olo = (o - ohi.astype(F32)).astype(BF16)
    mu = _dot(ohi, g256) + _dot(olo, g256)
    d = o - mu
    var = _dot((d * d).astype(BF16), g256)
    gate = gc_ref[...]
    oc_ref[...] = (d * lax.rsqrt(var + NORM_EPS) * rno_ref[...] * (gate * _sigmoid(gate))).astype(oc_ref.dtype)

    @pl.when(j == pl.num_programs(1) - 1)
    def _():
        sg_ref[0] = stg_ref[...]
        sr_ref[0] = str_ref[...]


def _recurrent(rec_in, sg0, sr0, lw, b, t, c):
    nchunk = t // c
    tri = jnp.tril(jnp.ones((c, c), F32)).astype(BF16)
    lg = jnp.log1p(-jnp.exp2(-5.0 - jnp.arange(H_C, dtype=F32)))
    pos = jnp.arange(c, dtype=F32)
    lg_v = jnp.repeat(lg, DV_C)
    lg_k = jnp.tile(jnp.repeat(lg, DK_C // 2), 2)
    dq = jnp.exp((pos[:, None] + 1.0) * lg_v[None, :])
    kdec = jnp.exp((c - 1.0 - pos)[:, None] * lg_k[None, :])
    diff = pos[:, None] - pos[None, :]
    dmat = jnp.where(diff >= 0, jnp.exp(jnp.maximum(diff, 0.0)[None] * lg[:, None, None]), 0.0)
    sdec = jnp.exp(c * lg_k)[None, :]
    hv = np.arange(W_B) // DV_B
    mg = jnp.asarray((hv[:, None] == (np.arange(H_B * DK_B) // DK_B)[None, :]).astype(np.float32))
    mr = jnp.asarray((hv[:, None] == ((np.arange(H_C * DK_C) % LANES) // (DK_C // 2))[None, :]).astype(np.float32))
    widths = (128, 128, 256, 128, 256, 256, 256, 256, 256)
    row = lambda w: pl.BlockSpec((c, w), lambda bb, j: (bb * nchunk + j, 0))
    per_b = lambda shape: pl.BlockSpec((1,) + shape, lambda bb, j: (bb, 0, 0))
    const = lambda a: pl.BlockSpec(a.shape, lambda bb, j: (0,) * a.ndim)
    consts = (tri, lw["g256"], lw["gno"], lw["rno"], dq, kdec, dmat, sdec, mg, mr)
    n = b * t
    return pl.pallas_call(
        _recurrent_kernel,
        grid=(b, nchunk),
        in_specs=[row(w) for w in widths] + [per_b(sg0.shape[1:]), per_b(sr0.shape[1:])] + [const(a) for a in consts],
        out_specs=[row(256), row(256), per_b(sg0.shape[1:]), per_b(sr0.shape[1:])],
        out_shape=[jax.ShapeDtypeStruct((n, 256), BF16), jax.ShapeDtypeStruct((n, 256), BF16),
                   jax.ShapeDtypeStruct(sg0.shape, F32), jax.ShapeDtypeStruct(sr0.shape, F32)],
        scratch_shapes=[pltpu.VMEM(sg0.shape[1:], F32), pltpu.VMEM(sr0.shape[1:], F32)],
        compiler_params=_params(("arbitrary", "arbitrary")),
        name="recurrent",
    )(*rec_in, sg0, sr0, *consts)


def _post_kernel(h_ref, oa_ref, ob_ref, oc_ref, p_ref, wo_ref, n2_ref, wu_ref, wd_ref, pn_ref,
                 wpg_ref, wpe_ref, out_ref):
    mix = jnp.concatenate([oa_ref[...], ob_ref[...], oc_ref[...]], axis=1)
    out_ref[...] = h_ref[...] + _dot(mix, wo_ref[...])

    def rms(x, g):
        return (x * lax.rsqrt(jnp.mean(x * x, axis=-1, keepdims=True) + NORM_EPS) * g).astype(BF16)

    xn = rms(out_ref[...], n2_ref[...])
    step = D_FF // 4
    for ci in range(4):
        u = jnp.maximum(_dot(xn, wu_ref[:, ci * step:(ci + 1) * step]), 0.0)
        out_ref[...] += _dot((u * u).astype(BF16), wd_ref[ci * step:(ci + 1) * step, :])
    h2 = out_ref[...]
    xn = rms(h2, pn_ref[...])
    gate = _sigmoid(_dot(xn, wpg_ref[...]))
    out_ref[...] = h2 + gate * _dot(p_ref[...].astype(BF16), wpe_ref[...])


def _post(h2d, oa, ob, oc, p2d, lw, tm):
    n = h2d.shape[0]
    row = lambda w: pl.BlockSpec((tm, w), lambda i: (i, 0))
    consts = (lw["wo"], lw["n2"], lw["wu"], lw["wd"], lw["pn"], lw["wpg"], lw["wpe"])
    const = lambda a: pl.BlockSpec(memory_space=pltpu.VMEM)
    return pl.pallas_call(
        _post_kernel,
        grid=(n // tm,),
        in_specs=[row(D_MODEL), row(W_A), row(W_B), row(W_C), row(D_PLE)] + [const(a) for a in consts],
        out_specs=row(D_MODEL),
        out_shape=jax.ShapeDtypeStruct((n, D_MODEL), F32),
        compiler_params=_params(("arbitrary",)),
        name="post",
    )(h2d, oa, ob, oc, p2d, *consts)


def _ret_perm():
    half = DK_C // 2
    return np.array([h * DK_C + part * half + j for part in range(2) for h in range(H_C) for j in range(half)])


def _layer_weights(l, norm1, w_in, b_f, fox_q_norm, fox_k_norm, gla_w_a2, gla_b_a, gla_o_norm,
                   ret_o_norm, w_out, norm2, w_up, w_down, ple_norm, w_pg, w_pe):
    (qa, ka, va, fa, qb, kb, vb, ab, gb, qc, kc, vc, gc) = jnp.split(w_in[l], SPLIT_IDX, axis=1)
    perm = _ret_perm()
    pad = lambda a: jnp.pad(a, ((0, 0), (0, LANES - a.shape[1])))
    wp = jnp.concatenate([qa, ka, va, pad(fa), qb, kb, vb, pad(ab), gb,
                          qc[:, perm], kc[:, perm], vc, gc], axis=1).astype(BF16)
    grp = np.arange(W_A) // HEAD_DIM
    g512 = jnp.asarray((grp[:, None] == grp[None, :]).astype(np.float32) / HEAD_DIM, BF16)
    row = lambda a: a.reshape(1, -1).astype(F32)
    return dict(
        g1=row(norm1[l]), wp=wp,
        bf=jnp.pad(row(b_f[l]), ((0, 0), (0, LANES - H_A))),
        gq=row(jnp.tile(fox_q_norm[l], H_A)) * (HEAD_DIM ** -0.5),
        gk=row(jnp.tile(fox_k_norm[l], H_A)),
        g512=g512, g256=g512[:W_B, :W_B],
        wa2=jnp.pad(gla_w_a2[l], ((0, LANES - GLA_RANK), (0, 0))).astype(BF16),
        ba=row(gla_b_a[l]),
        gno=row(gla_o_norm[l]), rno=row(ret_o_norm[l]),
        wo=w_out[l].astype(BF16), n2=row(norm2[l]), wu=w_up[l].astype(BF16), wd=w_down[l].astype(BF16),
        pn=row(ple_norm[l]), wpg=w_pg[l].astype(BF16), wpe=w_pe[l].astype(BF16),
    )


def _rotary_tables(pos):
    half = DK_C // 2
    freqs = ROPE_BASE ** (-jnp.arange(half, dtype=F32) / half)
    ang = pos[:, None] * freqs[None, :]
    return jnp.tile(jnp.cos(ang), (1, H_C)), jnp.tile(jnp.sin(ang), (1, H_C))


def _embed_states(s_gla, s_ret):
    b = s_gla.shape[0]
    eye_b = jnp.eye(H_B, dtype=F32)
    sg = jnp.einsum("bhkv,hg->bhvgk", s_gla.astype(F32), eye_b).reshape(b, W_B, H_B * DK_B)
    eye_c = jnp.eye(H_C, dtype=F32)
    half = DK_C // 2
    sr = s_ret.astype(F32).reshape(b, H_C, 2, half, DV_C)
    sr = jnp.einsum("bhpjv,hg->bhvpgj", sr, eye_c).reshape(b, W_C, H_C * DK_C)
    return sg, sr


def _extract_states(sg, sr):
    b = sg.shape[0]
    sg = sg.reshape(b, H_B, DV_B, H_B, DK_B)
    s_gla = jnp.stack([sg[:, h, :, h, :] for h in range(H_B)], axis=1)
    half = DK_C // 2
    sr = sr.reshape(b, H_C, DV_C, 2, H_C, half)
    s_ret = jnp.stack([sr[:, h, :, :, h, :] for h in range(H_C)], axis=1)
    s_ret = s_ret.reshape(b, H_C, DV_C, DK_C)
    return jnp.swapaxes(s_gla, 2, 3), jnp.swapaxes(s_ret, 2, 3)


def _decode_queries(qa):
    b, t, w = qa.shape
    head_col = np.arange(w) // HEAD_DIM
    mask = jnp.asarray(np.arange(H_A)[:, None] == head_col[None, :])
    return jnp.where(mask[None, None], qa[:, :, None, :], jnp.zeros((), qa.dtype)).reshape(b, t * H_A, w)


def kernel(x_prompt, x_sample, cache_k, cache_v, cache_logf, state_gla, state_ret, page_table,
           p_prompt, p_sample, norm1, w_in, b_f, fox_q_norm, fox_k_norm, gla_w_a2, gla_b_a,
           gla_o_norm, ret_o_norm, w_out, norm2, w_up, w_down, ple_norm, w_pg, w_pe):
    bp, sp, _ = x_prompt.shape
    db, ts, _ = x_sample.shape
    depth = w_in.shape[0]
    n_pool = cache_k.shape[1]
    past = page_table.shape[1] * PAGE_SIZE
    np_rows, ns_rows = bp * sp, db * ts
    tm_p = min(ROW_TILE, sp)
    tm_s = ns_rows
    att_t = min(ATT_TILE, sp)
    chunk_p = int(np.gcd(sp, GLA_CHUNK))
    chunk_s = int(np.gcd(ts, GLA_CHUNK))
    g_pages = min(PAGES_PER_STEP, page_table.shape[1])

    cos_p, sin_p = _rotary_tables(jnp.arange(sp, dtype=F32))
    cos_s, sin_s = _rotary_tables(past + jnp.arange(ts, dtype=F32))
    cos_s, sin_s = jnp.tile(cos_s, (db, 1)), jnp.tile(sin_s, (db, 1))

    ck = jnp.transpose(cache_k, (0, 1, 3, 4, 2))
    cv = jnp.transpose(cache_v, (0, 1, 3, 4, 2))
    clf_t = jnp.swapaxes(cache_logf, 2, 3)

    zero_g, zero_r = _embed_states(jnp.zeros((bp, H_B, DK_B, DV_B), F32), jnp.zeros((bp, H_C, DK_C, DV_C), F32))

    hp = x_prompt.reshape(np_rows, D_MODEL)
    hs = x_sample.reshape(ns_rows, D_MODEL)
    outs_p, outs_s = [], []
    for l in range(depth):
        lw = _layer_weights(l, norm1, w_in, b_f, fox_q_norm, fox_k_norm, gla_w_a2, gla_b_a, gla_o_norm,
                            ret_o_norm, w_out, norm2, w_up, w_down, ple_norm, w_pg, w_pe)

        (qa, k32, v32, kbf, vbf, logf, csum, *rec_in) = _inproj(hp, lw, cos_p, sin_p, tm_p, sp // tm_p)
        c_t = jnp.swapaxes(csum[:, :H_A].reshape(bp, sp, H_A), 1, 2).reshape(bp, H_A, 1, sp)
        oa = _flash(qa.reshape(bp, sp, W_A), kbf.reshape(bp, sp, W_A), vbf.reshape(bp, sp, W_A), c_t, att_t)
        ob, oc, sg, sr = _recurrent(rec_in, zero_g, zero_r, lw, bp, sp, chunk_p)
        hp = _post(hp, oa.reshape(np_rows, W_A), ob, oc, p_prompt[l].reshape(np_rows, D_PLE), lw, tm_p)
        s_gla, s_ret = _extract_states(sg, sr)
        outs_p.append((k32.reshape(bp, sp, H_A, HEAD_DIM), v32.reshape(bp, sp, H_A, HEAD_DIM),
                       logf[:, :H_A].reshape(bp, sp, H_A), s_gla, s_ret))

        (qa, k32, v32, kbf, vbf, logf, csum, *rec_in) = _inproj(hs, lw, cos_s, sin_s, tm_s, 1)
        pad_rows = lambda a: jnp.pad(jnp.swapaxes(a.reshape(db, ts, W_A), 1, 2), ((0, 0), (0, 0), (0, LANES - ts)))
        lf_new_t = jnp.pad(jnp.swapaxes(logf[:, :H_A].reshape(db, ts, H_A), 1, 2), ((0, 0), (0, 0), (0, LANES - ts)))
        oa = _decode(l, _decode_queries(qa.reshape(db, ts, W_A)), pad_rows(kbf), pad_rows(vbf), lf_new_t,
                     ck, cv, clf_t, page_table, g_pages)
        sg0, sr0 = _embed_states(state_gla[l], state_ret[l])
        ob, oc, sg, sr = _recurrent(rec_in, sg0, sr0, lw, db, ts, chunk_s)
        hs = _post(hs, oa.reshape(ns_rows, W_A), ob, oc, p_sample[l].reshape(ns_rows, D_PLE), lw, tm_s)
        s_gla, s_ret = _extract_states(sg, sr)
        outs_s.append((k32.reshape(db, ts, H_A, HEAD_DIM), v32.reshape(db, ts, H_A, HEAD_DIM),
                       logf[:, :H_A].reshape(db, ts, H_A), s_gla, s_ret))

    k_p, v_p, lf_p, gla_p, ret_p = [jnp.stack(a) for a in zip(*outs_p)]
    k_s, v_s, lf_s, gla_s, ret_s = [jnp.stack(a) for a in zip(*outs_s)]
    return (hp.reshape(bp, sp, D_MODEL), hs.reshape(db, ts, D_MODEL),
            k_p, v_p, lf_p, gla_p, ret_p, k_s, v_s, lf_s, gla_s, ret_s)
```

```python
import functools

import numpy as np
import jax
import jax.numpy as jnp
from jax import lax
from jax.experimental import pallas as pl
from jax.experimental.pallas import tpu as pltpu

F32 = jnp.float32
BF16 = jnp.bfloat16

D_MODEL = 1024
HEAD_DIM = 64
H_A = 8
W_A = H_A * HEAD_DIM
H_B = 4
DK_B = 32
DV_B = 64
W_B = H_B * DV_B
H_C = 4
DK_C = 64
DV_C = 64
W_C = H_C * DV_C
GLA_RANK = 16
GLA_TAU = 16.0
GLA_CHUNK = 64
RET_CHUNK = 64
D_FF = 4 * D_MODEL
D_PLE = 256
PAGE_SIZE = 128
ROPE_BASE = 10000.0
NORM_EPS = 1e-6
SPLIT_SIZES = (W_A, W_A, W_A, H_A,
               H_B * DK_B, H_B * DK_B, W_B, GLA_RANK, W_B,
               H_C * DK_C, H_C * DK_C, W_C, W_C)
SPLIT_IDX = tuple(int(i) for i in np.cumsum(SPLIT_SIZES)[:-1])

LANES = 128
AUG_W = H_A * LANES
LOG2E = 1.4426950408889634
NEG_BIG = -1e30
VMEM_LIMIT = 56 * 1024 * 1024

_OFF_QA, _OFF_KA, _OFF_VA, _OFF_FA = 0, 512, 1024, 1536
_OFF_QB, _OFF_KB, _OFF_VB, _OFF_AB, _OFF_GB = 1664, 1792, 1920, 2176, 2304
_OFF_QC, _OFF_KC, _OFF_VC, _OFF_GC = 2560, 2816, 3072, 3328
D_IN_PAD = 3584

ROW_TILE = 512
ATT_TILE = 1024
ATT_HEADS = 4
PAGES_PER_STEP = 8


def _dot(a, b):
    return jnp.dot(a, b, preferred_element_type=F32)


def _dot_nt(a, b):
    return lax.dot_general(a, b, (((1,), (1,)), ((), ())), preferred_element_type=F32)


def _dot_tn(a, b):
    return lax.dot_general(a, b, (((0,), (0,)), ((), ())), preferred_element_type=F32)


def _split3(x):
    hi = x.astype(BF16)
    r = x - hi.astype(F32)
    mid = r.astype(BF16)
    lo = (r - mid.astype(F32)).astype(BF16)
    return hi, mid, lo


def _dot3_left(w, x):
    hi, mid, lo = _split3(x)
    return _dot(w, hi) + _dot(w, mid) + _dot(w, lo)


def _dot3_right(x, w):
    hi, mid, lo = _split3(x)
    return _dot(hi, w) + _dot(mid, w) + _dot(lo, w)


def _log_sigmoid(x):
    return jnp.minimum(x, 0.0) - jnp.log1p(jnp.exp(-jnp.abs(x)))


def _sigmoid(x):
    return 1.0 / (1.0 + jnp.exp(-x))


def _params(sem):
    return pltpu.CompilerParams(dimension_semantics=sem, vmem_limit_bytes=VMEM_LIMIT)


def _const_spec(shape):
    nd = len(shape)
    return pl.BlockSpec(shape, lambda *_: (0,) * nd)


def _inproj_kernel(x_ref, g1_ref, wp_ref, bf_ref, gq_ref, gk_ref, g512_ref, wa2_ref, ba_ref,
                   cos_ref, sin_ref, tri_ref, place_ref, placec_ref, qone_ref,
                   qa_ref, k32_ref, v32_ref, kbf_ref, vbf_ref, logf_ref, c_ref,
                   qb_ref, kb_ref, vb_ref, la_ref, gb_ref,
                   qc_ref, kc_ref, vc_ref, gc_ref,
                   qaug_ref, kaug_ref, vt_ref,
                   carry_ref, *, blocks_per_seq):
    i = pl.program_id(0)
    x = x_ref[...]
    ms = jnp.mean(x * x, axis=-1, keepdims=True)
    xn = (x * lax.rsqrt(ms + NORM_EPS) * g1_ref[...]).astype(BF16)

    za = _dot(xn, wp_ref[:, _OFF_QA:_OFF_QB])
    g512 = g512_ref[...]

    def head_rms(t, gain):
        msq = _dot((t * t).astype(BF16), g512)
        return t * lax.rsqrt(msq + NORM_EPS) * gain

    qn = head_rms(za[:, 0:512], gq_ref[...])
    qa_ref[...] = qn.astype(BF16)
    k = head_rms(za[:, 512:1024], gk_ref[...])
    k32_ref[...] = k
    kbf = k.astype(BF16)
    kbf_ref[...] = kbf
    v = za[:, 1024:1536]
    v32_ref[...] = v
    vbf_ref[...] = v.astype(BF16)
    vt_ref[...] = v.T.astype(BF16)
    logf = _log_sigmoid(za[:, 1536:1664] + bf_ref[...])
    logf_ref[...] = logf

    @pl.when(i % blocks_per_seq == 0)
    def _():
        carry_ref[...] = jnp.zeros_like(carry_ref)

    c = _dot3_left(tri_ref[...], logf) + carry_ref[...]
    c_ref[...] = c
    carry_ref[...] = c[c.shape[0] - 1:, :]

    place = place_ref[...]
    qaug_ref[...] = (_dot((qn * LOG2E).astype(BF16), place) + qone_ref[...]).astype(BF16)
    chi, cmid, clo = _split3(c * (-LOG2E))
    kaug = (_dot(kbf, place) + _dot(chi, placec_ref[0]) + _dot(cmid, placec_ref[1]) + _dot(clo, placec_ref[2]))
    kaug_ref[...] = kaug.astype(BF16)

    zb = _dot(xn, wp_ref[:, _OFF_QB:_OFF_QC])
    qb_ref[...] = zb[:, 0:128]
    kb_ref[...] = zb[:, 128:256]
    vb_ref[...] = zb[:, 256:512].astype(BF16)
    ab = zb[:, 512:640].astype(BF16)
    la_ref[...] = _log_sigmoid(_dot(ab, wa2_ref[...]) + ba_ref[...]) * (1.0 / GLA_TAU)
    gb_ref[...] = zb[:, 640:896]

    zc = _dot(xn, wp_ref[:, _OFF_QC:D_IN_PAD])
    cos = cos_ref[...]
    sin = sin_ref[...]

    def rot(t):
        t1, t2 = t[:, 0:128], t[:, 128:256]
        return jnp.concatenate([t1 * cos - t2 * sin, t1 * sin + t2 * cos], axis=1)

    qc_ref[...] = rot(zc[:, 0:256])
    kc_ref[...] = rot(zc[:, 256:512]) * (DK_C ** -0.5)
    vc_ref[...] = zc[:, 512:768].astype(BF16)
    gc_ref[...] = zc[:, 768:1024]


def _inproj(x2d, lw, cos_tab, sin_tab, tm, blocks_per_seq):
    n = x2d.shape[0]
    nb_tab = cos_tab.shape[0] // tm
    row = lambda w: pl.BlockSpec((tm, w), lambda i: (i, 0))
    tab = pl.BlockSpec((tm, LANES), lambda i: (i % nb_tab, 0))
    tri = jnp.tril(jnp.ones((tm, tm), F32)).astype(BF16)
    widths = (512, 512, 512, 512, 512, 128, 128, 128, 128, 256, 128, 256, 256, 256, 256, 256, AUG_W, AUG_W)
    dtypes = (BF16, F32, F32, BF16, BF16, F32, F32, F32, F32, BF16, F32, F32, F32, F32, BF16, F32, BF16, BF16)
    consts = (lw["g1"], lw["wp"], lw["bf"], lw["gq"], lw["gk"], lw["g512"], lw["wa2"], lw["ba"])
    place_np = np.zeros((W_A, AUG_W), np.float32)
    place_np[np.arange(W_A), (np.arange(W_A) // HEAD_DIM) * LANES + np.arange(W_A) % HEAD_DIM] = 1.0
    placec_np = np.zeros((3, LANES, AUG_W), np.float32)
    qone_np = np.zeros((1, AUG_W), np.float32)
    for piece in range(3):
        placec_np[piece, np.arange(H_A), np.arange(H_A) * LANES + HEAD_DIM + piece] = 1.0
        qone_np[0, np.arange(H_A) * LANES + HEAD_DIM + piece] = 1.0
    extra = (jnp.asarray(place_np, BF16), jnp.asarray(placec_np, BF16), jnp.asarray(qone_np))
    return pl.pallas_call(
        functools.partial(_inproj_kernel, blocks_per_seq=blocks_per_seq),
        grid=(n // tm,),
        in_specs=([row(D_MODEL)] + [_const_spec(a.shape) for a in consts] + [tab, tab, _const_spec(tri.shape)]
                  + [_const_spec(a.shape) for a in extra]),
        out_specs=[row(w) for w in widths] + [pl.BlockSpec((W_A, tm), lambda i: (0, i))],
        out_shape=([jax.ShapeDtypeStruct((n, w), d) for w, d in zip(widths, dtypes)]
                   + [jax.ShapeDtypeStruct((W_A, n), BF16)]),
        scratch_shapes=[pltpu.VMEM((1, LANES), F32)],
        compiler_params=_params(("arbitrary",)),
        name="inproj",
    )(x2d, *consts, cos_tab, sin_tab, tri, *extra)


def _flash_kernel(qi_ref, ki_ref, q_ref, k_ref, vt_ref, o_ref, m_ref, l_ref, acc_ref, *, heads):
    step = pl.program_id(2)
    qi = qi_ref[step]
    ki = ki_ref[step]
    tq = q_ref.shape[1]
    tk = k_ref.shape[1]

    @pl.when(ki == 0)
    def _():
        m_ref[...] = jnp.full_like(m_ref, NEG_BIG)
        l_ref[...] = jnp.zeros_like(l_ref)
        acc_ref[...] = jnp.zeros_like(acc_ref)

    def body(diagonal):
        if diagonal:
            keep = (lax.broadcasted_iota(jnp.int32, (tk, tq), 0)
                    <= lax.broadcasted_iota(jnp.int32, (tk, tq), 1))
        for hh in range(heads):
            qh = q_ref[0, :, hh * LANES:(hh + 1) * LANES]
            kh = k_ref[0, :, hh * LANES:(hh + 1) * LANES]
            st = _dot_nt(kh, qh)
            if diagonal:
                st = jnp.where(keep, st, NEG_BIG)
            m_prev = m_ref[hh]
            m_new = jnp.maximum(m_prev, jnp.max(st, axis=0, keepdims=True))
            alpha = jnp.exp2(m_prev - m_new)
            p = jnp.exp2(st - m_new)
            l_ref[hh] = alpha * l_ref[hh] + jnp.sum(p, axis=0, keepdims=True)
            vth = vt_ref[hh * HEAD_DIM:(hh + 1) * HEAD_DIM, :]
            acc_ref[hh] = alpha * acc_ref[hh] + _dot(vth, p.astype(BF16))
            m_ref[hh] = m_new

    @pl.when(ki < qi)
    def _():
        body(False)

    @pl.when(ki == qi)
    def _():
        body(True)
        ot = jnp.concatenate([acc_ref[hh] / l_ref[hh] for hh in range(heads)], axis=0)
        o_ref[0] = ot.T.astype(o_ref.dtype)


def _flash(q_aug, k_aug, v_t, t, heads):
    b, s, _ = q_aug.shape
    nq = s // t
    qi_np = np.array([i for i in range(nq) for _ in range(i + 1)], np.int32)
    ki_np = np.array([j for i in range(nq) for j in range(i + 1)], np.int32)
    grid_spec = pltpu.PrefetchScalarGridSpec(
        num_scalar_prefetch=2,
        grid=(b, H_A // heads, len(qi_np)),
        in_specs=[
            pl.BlockSpec((1, t, heads * LANES), lambda bb, g, st, qi, ki: (bb, qi[st], g)),
            pl.BlockSpec((1, t, heads * LANES), lambda bb, g, st, qi, ki: (bb, ki[st], g)),
            pl.BlockSpec((heads * HEAD_DIM, t), lambda bb, g, st, qi, ki: (g, bb * nq + ki[st])),
        ],
        out_specs=pl.BlockSpec((1, t, heads * HEAD_DIM), lambda bb, g, st, qi, ki: (bb, qi[st], g)),
        scratch_shapes=[pltpu.VMEM((heads, 1, t), F32), pltpu.VMEM((heads, 1, t), F32),
                        pltpu.VMEM((heads, HEAD_DIM, t), F32)],
    )
    return pl.pallas_call(
        functools.partial(_flash_kernel, heads=heads),
        grid_spec=grid_spec,
        out_shape=jax.ShapeDtypeStruct((b, s, W_A), BF16),
        compiler_params=_params(("arbitrary", "arbitrary", "arbitrary")),
        name="fox_prompt",
    )(jnp.asarray(qi_np), jnp.asarray(ki_np), q_aug, k_aug, v_t)


def _decode_kernel(pt_ref, qm_ref, kn_ref, vn_ref, lfn_ref, w_ref, *rest, n_pages):
    g = n_pages
    k_refs = rest[0:g]
    v_refs = rest[g:2 * g]
    lf_refs = rest[2 * g:3 * g]
    o_ref, m_ref, l_ref, acc_ref, carry_ref = rest[3 * g:]
    j = pl.program_id(1)
    qm = qm_ref[0]
    rows = qm.shape[0]

    def attend(bias, kt, vt):
        s = _dot(qm, kt) + bias
        m_prev = m_ref[...]
        m_new = jnp.maximum(m_prev, jnp.max(s, axis=1, keepdims=True))
        alpha = jnp.exp(m_prev - m_new)
        p = jnp.exp(s - m_new)
        l_ref[...] = alpha * l_ref[...] + jnp.sum(p, axis=1, keepdims=True)
        acc_ref[...] = alpha * acc_ref[...] + _dot_nt(p.astype(BF16), vt)
        m_ref[...] = m_new

    w = w_ref[...]

    @pl.when(j == 0)
    def _():
        m_ref[...] = jnp.full_like(m_ref, NEG_BIG)
        l_ref[...] = jnp.zeros_like(l_ref)
        acc_ref[...] = jnp.zeros_like(acc_ref)
        carry_ref[...] = jnp.zeros_like(carry_ref)
        y = _dot3_right(lfn_ref[0], w)
        cum = y[:, LANES:] - y[:, :LANES]
        bias = jnp.concatenate([-cum] * (rows // H_A), axis=0)
        tok = lax.broadcasted_iota(jnp.int32, (rows, LANES), 0) // H_A
        col = lax.broadcasted_iota(jnp.int32, (rows, LANES), 1)
        attend(jnp.where(col <= tok, bias, NEG_BIG), kn_ref[0], vn_ref[0])

    x = jnp.concatenate([r[...] for r in lf_refs], axis=0)
    y = _dot3_right(x, w)
    off = carry_ref[...]
    pieces = []
    for i in range(g):
        pieces.append(y[H_A * i:H_A * (i + 1), :LANES] + off)
        off = off + y[H_A * i:H_A * (i + 1), LANES:]
    carry_ref[...] = off
    bias_t = jnp.concatenate(pieces, axis=1)
    bias = jnp.concatenate([bias_t] * (rows // H_A), axis=0)
    page_t = lambda r: r[...].reshape(W_A, PAGE_SIZE).astype(BF16)
    kt = jnp.concatenate([page_t(r) for r in k_refs], axis=1)
    vt = jnp.concatenate([page_t(r) for r in v_refs], axis=1)
    attend(bias, kt, vt)

    @pl.when(j == pl.num_programs(1) - 1)
    def _():
        o = acc_ref[...] / l_ref[...]
        head_row = lax.broadcasted_iota(jnp.int32, o.shape, 0) % H_A
        head_col = lax.broadcasted_iota(jnp.int32, o.shape, 1) // HEAD_DIM
        o = jnp.where(head_row == head_col, o, 0.0)
        t = rows // H_A
        o_ref[0] = jnp.sum(o.reshape(t, H_A, o.shape[1]), axis=1).astype(o_ref.dtype)


def _decode(layer, qm, k_new, v_new, lf_new_t, cache_k, cache_v, cache_lf_t, page_table, g):
    db, n_pages = page_table.shape
    t = qm.shape[1] // H_A
    steps = n_pages // g
    suffix = (np.arange(LANES)[:, None] > np.arange(LANES)[None, :]).astype(np.float32)
    w = jnp.asarray(np.concatenate([suffix, np.ones((LANES, LANES), np.float32)], axis=1), BF16)

    def page_map(i, trailing):
        def index_map(b, j, pt):
            return (layer, pt[b, n_pages - 1 - (j * g + i)]) + (0,) * trailing
        return index_map

    seq = lambda shape: pl.BlockSpec((1,) + shape, lambda b, j, pt: (b, 0, 0))
    in_specs = [seq(qm.shape[1:]), seq((W_A, LANES)), seq((W_A, LANES)), seq((H_A, LANES)),
                pl.BlockSpec(w.shape, lambda b, j, pt: (0, 0))]
    kv_page = (None, None, H_A, HEAD_DIM, PAGE_SIZE)
    in_specs += [pl.BlockSpec(kv_page, page_map(i, 3)) for i in range(g)]
    in_specs += [pl.BlockSpec(kv_page, page_map(i, 3)) for i in range(g)]
    in_specs += [pl.BlockSpec((None, None, H_A, PAGE_SIZE), page_map(i, 2)) for i in range(g)]
    rows = qm.shape[1]
    grid_spec = pltpu.PrefetchScalarGridSpec(
        num_scalar_prefetch=1,
        grid=(db, steps),
        in_specs=in_specs,
        out_specs=pl.BlockSpec((1, t, W_A), lambda b, j, pt: (b, 0, 0)),
        scratch_shapes=[pltpu.VMEM((rows, 1), F32), pltpu.VMEM((rows, 1), F32),
                        pltpu.VMEM((rows, W_A), F32), pltpu.VMEM((H_A, LANES), F32)],
    )
    return pl.pallas_call(
        functools.partial(_decode_kernel, n_pages=g),
        grid_spec=grid_spec,
        out_shape=jax.ShapeDtypeStruct((db, t, W_A), BF16),
        compiler_params=_params(("arbitrary", "arbitrary")),
        name="fox_sample",
    )(page_table, qm, k_new, v_new, lf_new_t, w,
      *([cache_k] * g), *([cache_v] * g), *([cache_lf_t] * g))


def _recurrent_kernel(qb_ref, kb_ref, vb_ref, la_ref, gb_ref, qc_ref, kc_ref, vc_ref, gc_ref,
                      sg0_ref, sr0_ref, tri_ref, g256_ref, gno_ref, rno_ref,
                      dq_ref, kdec_ref, dmat_ref, sdec_ref, mg_ref, mr_ref,
                      ob_ref, oc_ref, sg_ref, sr_ref, stg_ref, str_ref):
    j = pl.program_id(1)
    c = qb_ref.shape[0]

    @pl.when(j == 0)
    def _():
        stg_ref[...] = sg0_ref[0]
        str_ref[...] = sr0_ref[0]

    tri = tri_ref[...]
    causal = (lax.broadcasted_iota(jnp.int32, (c, c), 1) <= lax.broadcasted_iota(jnp.int32, (c, c), 0))
    lane_k = lax.broadcasted_iota(jnp.int32, (1, LANES), 1)
    lane_v = lax.broadcasted_iota(jnp.int32, (1, 2 * LANES), 1)
    g256 = g256_ref[...]

    la = la_ref[...]
    hi = la.astype(BF16)
    lo = (la - hi.astype(F32)).astype(BF16)
    b = _dot(tri, hi) + _dot(tri, lo)
    b_last = b[c - 1:, :]
    qs = (qb_ref[...] * (jnp.exp(b) * (DK_B ** -0.5))).astype(BF16)
    kinv = (kb_ref[...] * jnp.exp(-b)).astype(BF16)
    kend = (kb_ref[...] * jnp.exp(b_last - b)).astype(BF16)
    vb = vb_ref[...]
    st = stg_ref[...]
    o = _dot_nt(qs, st.astype(BF16))
    for h in range(H_B):
        qh = jnp.where(lane_k // DK_B == h, qs, jnp.zeros_like(qs))
        att = jnp.where(causal, _dot_nt(qh, kinv), 0.0)
        vh = jnp.where(lane_v // DV_B == h, vb, jnp.zeros_like(vb))
        o = o + _dot(att.astype(BF16), vh)
    stg_ref[...] = st * jnp.exp(b_last) + _dot_tn(vb, kend) * mg_ref[...]
    msq = _dot((o * o).astype(BF16), g256)
    gate = gb_ref[...]
    ob_ref[...] = (o * lax.rsqrt(msq + NORM_EPS) * gno_ref[...] * (gate * _sigmoid(gate))).astype(ob_ref.dtype)

    q = qc_ref[...].astype(BF16)
    kc = kc_ref[...]
    vc = vc_ref[...]
    st = str_ref[...]
    o = _dot_nt(q, st.astype(BF16)) * dq_ref[...]
    kbf = kc.astype(BF16)
    for h in range(H_C):
        qh = jnp.where((lane_v % LANES) // (DK_C // 2) == h, q, jnp.zeros_like(q))
        att = _dot_nt(qh, kbf) * dmat_ref[h]
        vh = jnp.where(lane_v // DV_C == h, vc, jnp.zeros_like(vc))
        o = o + _dot(att.astype(BF16), vh)
    str_ref[...] = st * sdec_ref[...] + _dot_tn(vc, (kc * kdec_ref[...]).astype(BF16)) * mr_ref[...]
    ohi = o.astype(BF16)
    olo = (o - ohi.astype(F32)).astype(BF16)
    mu = _dot(ohi, g256) + _dot(olo, g256)
    d = o - mu
    var = _dot((d * d).astype(BF16), g256)
    gate = gc_ref[...]
    oc_ref[...] = (d * lax.rsqrt(var + NORM_EPS) * rno_ref[...] * (gate * _sigmoid(gate))).astype(oc_ref.dtype)

    @pl.when(j == pl.num_programs(1) - 1)
    def _():
        sg_ref[0] = stg_ref[...]
        sr_ref[0] = str_ref[...]


def _recurrent(rec_in, sg0, sr0, lw, b, t, c):
    nchunk = t // c
    tri = jnp.tril(jnp.ones((c, c), F32)).astype(BF16)
    lg = jnp.log1p(-jnp.exp2(-5.0 - jnp.arange(H_C, dtype=F32)))
    pos = jnp.arange(c, dtype=F32)
    lg_v = jnp.repeat(lg, DV_C)
    lg_k = jnp.tile(jnp.repeat(lg, DK_C // 2), 2)
    dq = jnp.exp((pos[:, None] + 1.0) * lg_v[None, :])
    kdec = jnp.exp((c - 1.0 - pos)[:, None] * lg_k[None, :])
    diff = pos[:, None] - pos[None, :]
    dmat = jnp.where(diff >= 0, jnp.exp(jnp.maximum(diff, 0.0)[None] * lg[:, None, None]), 0.0)
    sdec = jnp.exp(c * lg_k)[None, :]
    hv = np.arange(W_B) // DV_B
    mg = jnp.asarray((hv[:, None] == (np.arange(H_B * DK_B) // DK_B)[None, :]).astype(np.float32))
    mr = jnp.asarray((hv[:, None] == ((np.arange(H_C * DK_C) % LANES) // (DK_C // 2))[None, :]).astype(np.float32))
    widths = (128, 128, 256, 128, 256, 256, 256, 256, 256)
    row = lambda w: pl.BlockSpec((c, w), lambda bb, j: (bb * nchunk + j, 0))
    per_b = lambda shape: pl.BlockSpec((1,) + shape, lambda bb, j: (bb, 0, 0))
    const = lambda a: pl.BlockSpec(a.shape, lambda bb, j: (0,) * a.ndim)
    consts = (tri, lw["g256"], lw["gno"], lw["rno"], dq, kdec, dmat, sdec, mg, mr)
    n = b * t
    return pl.pallas_call(
        _recurrent_kernel,
        grid=(b, nchunk),
        in_specs=[row(w) for w in widths] + [per_b(sg0.shape[1:]), per_b(sr0.shape[1:])] + [const(a) for a in consts],
        out_specs=[row(256), row(256), per_b(sg0.shape[1:]), per_b(sr0.shape[1:])],
        out_shape=[jax.ShapeDtypeStruct((n, 256), BF16), jax.ShapeDtypeStruct((n, 256), BF16),
                   jax.ShapeDtypeStruct(sg0.shape, F32), jax.ShapeDtypeStruct(sr0.shape, F32)],
        scratch_shapes=[pltpu.VMEM(sg0.shape[1:], F32), pltpu.VMEM(sr0.shape[1:], F32)],
        compiler_params=_params(("arbitrary", "arbitrary")),
        name="recurrent",
    )(*rec_in, sg0, sr0, *consts)


def _post_kernel(h_ref, oa_ref, ob_ref, oc_ref, p_ref, wo_ref, n2_ref, wu_ref, wd_ref, pn_ref,
                 wpg_ref, wpe_ref, out_ref):
    mix = jnp.concatenate([oa_ref[...], ob_ref[...], oc_ref[...]], axis=1)
    out_ref[...] = h_ref[...] + _dot(mix, wo_ref[...])

    def rms(x, g):
        return (x * lax.rsqrt(jnp.mean(x * x, axis=-1, keepdims=True) + NORM_EPS) * g).astype(BF16)

    xn = rms(out_ref[...], n2_ref[...])
    step = D_FF // 4
    for ci in range(4):
        u = jnp.maximum(_dot(xn, wu_ref[:, ci * step:(ci + 1) * step]), 0.0)
        out_ref[...] += _dot((u * u).astype(BF16), wd_ref[ci * step:(ci + 1) * step, :])
    h2 = out_ref[...]
    xn = rms(h2, pn_ref[...])
    gate = _sigmoid(_dot(xn, wpg_ref[...]))
    out_ref[...] = h2 + gate * _dot(p_ref[...].astype(BF16), wpe_ref[...])


def _post(h2d, oa, ob, oc, p2d, lw, tm):
    n = h2d.shape[0]
    row = lambda w: pl.BlockSpec((tm, w), lambda i: (i, 0))
    consts = (lw["wo"], lw["n2"], lw["wu"], lw["wd"], lw["pn"], lw["wpg"], lw["wpe"])
    const = lambda a: pl.BlockSpec(memory_space=pltpu.VMEM)
    return pl.pallas_call(
        _post_kernel,
        grid=(n // tm,),
        in_specs=[row(D_MODEL), row(W_A), row(W_B), row(W_C), row(D_PLE)] + [const(a) for a in consts],
        out_specs=row(D_MODEL),
        out_shape=jax.ShapeDtypeStruct((n, D_MODEL), F32),
        compiler_params=_params(("arbitrary",)),
        name="post",
    )(h2d, oa, ob, oc, p2d, *consts)


def _ret_perm():
    half = DK_C // 2
    return np.array([h * DK_C + part * half + j for part in range(2) for h in range(H_C) for j in range(half)])


def _layer_weights(l, norm1, w_in, b_f, fox_q_norm, fox_k_norm, gla_w_a2, gla_b_a, gla_o_norm,
                   ret_o_norm, w_out, norm2, w_up, w_down, ple_norm, w_pg, w_pe):
    (qa, ka, va, fa, qb, kb, vb, ab, gb, qc, kc, vc, gc) = jnp.split(w_in[l], SPLIT_IDX, axis=1)
    perm = _ret_perm()
    pad = lambda a: jnp.pad(a, ((0, 0), (0, LANES - a.shape[1])))
    wp = jnp.concatenate([qa, ka, va, pad(fa), qb, kb, vb, pad(ab), gb,
                          qc[:, perm], kc[:, perm], vc, gc], axis=1).astype(BF16)
    grp = np.arange(W_A) // HEAD_DIM
    g512 = jnp.asarray((grp[:, None] == grp[None, :]).astype(np.float32) / HEAD_DIM, BF16)
    row = lambda a: a.reshape(1, -1).astype(F32)
    return dict(
        g1=row(norm1[l]), wp=wp,
        bf=jnp.pad(row(b_f[l]), ((0, 0), (0, LANES - H_A))),
        gq=row(jnp.tile(fox_q_norm[l], H_A)) * (HEAD_DIM ** -0.5),
        gk=row(jnp.tile(fox_k_norm[l], H_A)),
        g512=g512, g256=g512[:W_B, :W_B],
        wa2=jnp.pad(gla_w_a2[l], ((0, LANES - GLA_RANK), (0, 0))).astype(BF16),
        ba=row(gla_b_a[l]),
        gno=row(gla_o_norm[l]), rno=row(ret_o_norm[l]),
        wo=w_out[l].astype(BF16), n2=row(norm2[l]), wu=w_up[l].astype(BF16), wd=w_down[l].astype(BF16),
        pn=row(ple_norm[l]), wpg=w_pg[l].astype(BF16), wpe=w_pe[l].astype(BF16),
    )


def _rotary_tables(pos):
    half = DK_C // 2
    freqs = ROPE_BASE ** (-jnp.arange(half, dtype=F32) / half)
    ang = pos[:, None] * freqs[None, :]
    return jnp.tile(jnp.cos(ang), (1, H_C)), jnp.tile(jnp.sin(ang), (1, H_C))


def _embed_states(s_gla, s_ret):
    b = s_gla.shape[0]
    eye_b = jnp.eye(H_B, dtype=F32)
    sg = jnp.einsum("bhkv,hg->bhvgk", s_gla.astype(F32), eye_b).reshape(b, W_B, H_B * DK_B)
    eye_c = jnp.eye(H_C, dtype=F32)
    half = DK_C // 2
    sr = s_ret.astype(F32).reshape(b, H_C, 2, half, DV_C)
    sr = jnp.einsum("bhpjv,hg->bhvpgj", sr, eye_c).reshape(b, W_C, H_C * DK_C)
    return sg, sr


def _extract_states(sg, sr):
    b = sg.shape[0]
    sg = sg.reshape(b, H_B, DV_B, H_B, DK_B)
    s_gla = jnp.stack([sg[:, h, :, h, :] for h in range(H_B)], axis=1)
    half = DK_C // 2
    sr = sr.reshape(b, H_C, DV_C, 2, H_C, half)
    s_ret = jnp.stack([sr[:, h, :, :, h, :] for h in range(H_C)], axis=1)
    s_ret = s_ret.reshape(b, H_C, DV_C, DK_C)
    return jnp.swapaxes(s_gla, 2, 3), jnp.swapaxes(s_ret, 2, 3)


def _decode_queries(qa):
    b, t, w = qa.shape
    head_col = np.arange(w) // HEAD_DIM
    mask = jnp.asarray(np.arange(H_A)[:, None] == head_col[None, :])
    return jnp.where(mask[None, None], qa[:, :, None, :], jnp.zeros((), qa.dtype)).reshape(b, t * H_A, w)


def kernel(x_prompt, x_sample, cache_k, cache_v, cache_logf, state_gla, state_ret, page_table,
           p_prompt, p_sample, norm1, w_in, b_f, fox_q_norm, fox_k_norm, gla_w_a2, gla_b_a,
           gla_o_norm, ret_o_norm, w_out, norm2, w_up, w_down, ple_norm, w_pg, w_pe):
    bp, sp, _ = x_prompt.shape
    db, ts, _ = x_sample.shape
    depth = w_in.shape[0]
    n_pool = cache_k.shape[1]
    past = page_table.shape[1] * PAGE_SIZE
    np_rows, ns_rows = bp * sp, db * ts
    tm_p = min(ROW_TILE, sp)
    tm_s = ns_rows
    att_t = min(ATT_TILE, sp)
    chunk_p = int(np.gcd(sp, GLA_CHUNK))
    chunk_s = int(np.gcd(ts, GLA_CHUNK))
    g_pages = min(PAGES_PER_STEP, page_table.shape[1])

    cos_p, sin_p = _rotary_tables(jnp.arange(sp, dtype=F32))
    cos_s, sin_s = _rotary_tables(past + jnp.arange(ts, dtype=F32))
    cos_s, sin_s = jnp.tile(cos_s, (db, 1)), jnp.tile(sin_s, (db, 1))

    ck = jnp.transpose(cache_k, (0, 1, 3, 4, 2))
    cv = jnp.transpose(cache_v, (0, 1, 3, 4, 2))
    clf_t = jnp.swapaxes(cache_logf, 2, 3)

    zero_g, zero_r = _embed_states(jnp.zeros((bp, H_B, DK_B, DV_B), F32), jnp.zeros((bp, H_C, DK_C, DV_C), F32))

    hp = x_prompt.reshape(np_rows, D_MODEL)
    hs = x_sample.reshape(ns_rows, D_MODEL)
    outs_p, outs_s = [], []
    for l in range(depth):
        lw = _layer_weights(l, norm1, w_in, b_f, fox_q_norm, fox_k_norm, gla_w_a2, gla_b_a, gla_o_norm,
                            ret_o_norm, w_out, norm2, w_up, w_down, ple_norm, w_pg, w_pe)

        (qa, k32, v32, kbf, vbf, logf, csum, *rec_in, q_aug, k_aug, v_t) = _inproj(
            hp, lw, cos_p, sin_p, tm_p, sp // tm_p)
        oa = _flash(q_aug.reshape(bp, sp, AUG_W), k_aug.reshape(bp, sp, AUG_W), v_t, att_t, ATT_HEADS)
        ob, oc, sg, sr = _recurrent(rec_in, zero_g, zero_r, lw, bp, sp, chunk_p)
        hp = _post(hp, oa.reshape(np_rows, W_A), ob, oc, p_prompt[l].reshape(np_rows, D_PLE), lw, tm_p)
        s_gla, s_ret = _extract_states(sg, sr)
        outs_p.append((k32.reshape(bp, sp, H_A, HEAD_DIM), v32.reshape(bp, sp, H_A, HEAD_DIM),
                       logf[:, :H_A].reshape(bp, sp, H_A), s_gla, s_ret))

        (qa, k32, v32, kbf, vbf, logf, csum, *rec_in, _, _, _) = _inproj(hs, lw, cos_s, sin_s, tm_s, 1)
        pad_rows = lambda a: jnp.pad(jnp.swapaxes(a.reshape(db, ts, W_A), 1, 2), ((0, 0), (0, 0), (0, LANES - ts)))
        lf_new_t = jnp.pad(jnp.swapaxes(logf[:, :H_A].reshape(db, ts, H_A), 1, 2), ((0, 0), (0, 0), (0, LANES - ts)))
        oa = _decode(l, _decode_queries(qa.reshape(db, ts, W_A)), pad_rows(kbf), pad_rows(vbf), lf_new_t,
                     ck, cv, clf_t, page_table, g_pages)
        sg0, sr0 = _embed_states(state_gla[l], state_ret[l])
        ob, oc, sg, sr = _recurrent(rec_in, sg0, sr0, lw, db, ts, chunk_s)
        hs = _post(hs, oa.reshape(ns_rows, W_A), ob, oc, p_sample[l].reshape(ns_rows, D_PLE), lw, tm_s)
        s_gla, s_ret = _extract_states(sg, sr)
        outs_s.append((k32.reshape(db, ts, H_A, HEAD_DIM), v32.reshape(db, ts, H_A, HEAD_DIM),
                       logf[:, :H_A].reshape(db, ts, H_A), s_gla, s_ret))

    k_p, v_p, lf_p, gla_p, ret_p = [jnp.stack(a) for a in zip(*outs_p)]
    k_s, v_s, lf_s, gla_s, ret_s = [jnp.stack(a) for a in zip(*outs_s)]
    return (hp.reshape(bp, sp, D_MODEL), hs.reshape(db, ts, D_MODEL),
            k_p, v_p, lf_p, gla_p, ret_p, k_s, v_s, lf_s, gla_s, ret_s)
```

```python
import functools

import numpy as np
import jax
import jax.numpy as jnp
from jax import lax
from jax.experimental import pallas as pl
from jax.experimental.pallas import tpu as pltpu

F32 = jnp.float32
BF16 = jnp.bfloat16

D_MODEL = 1024
HEAD_DIM = 64
H_A = 8
W_A = H_A * HEAD_DIM
H_B = 4
DK_B = 32
DV_B = 64
W_B = H_B * DV_B
H_C = 4
DK_C = 64
DV_C = 64
W_C = H_C * DV_C
GLA_RANK = 16
GLA_TAU = 16.0
GLA_CHUNK = 64
RET_CHUNK = 64
D_FF = 4 * D_MODEL
D_PLE = 256
PAGE_SIZE = 128
ROPE_BASE = 10000.0
NORM_EPS = 1e-6
SPLIT_SIZES = (W_A, W_A, W_A, H_A,
               H_B * DK_B, H_B * DK_B, W_B, GLA_RANK, W_B,
               H_C * DK_C, H_C * DK_C, W_C, W_C)
SPLIT_IDX = tuple(int(i) for i in np.cumsum(SPLIT_SIZES)[:-1])

LANES = 128
AUG_W = H_A * LANES
LOG2E = 1.4426950408889634
NEG_BIG = -1e30
VMEM_LIMIT = 56 * 1024 * 1024

_OFF_QA, _OFF_KA, _OFF_VA, _OFF_FA = 0, 512, 1024, 1536
_OFF_QB, _OFF_KB, _OFF_VB, _OFF_AB, _OFF_GB = 1664, 1792, 1920, 2176, 2304
_OFF_QC, _OFF_KC, _OFF_VC, _OFF_GC = 2560, 2816, 3072, 3328
D_IN_PAD = 3584

ROW_TILE = 512
ATT_TILE = 1024
ATT_HEADS = 4
REC_CHUNKS_PER_STEP = 4
PAGES_PER_STEP = 8


def _dot(a, b):
    return jnp.dot(a, b, preferred_element_type=F32)


def _dot_nt(a, b):
    return lax.dot_general(a, b, (((1,), (1,)), ((), ())), preferred_element_type=F32)


def _dot_tn(a, b):
    return lax.dot_general(a, b, (((0,), (0,)), ((), ())), preferred_element_type=F32)


def _split3(x):
    hi = x.astype(BF16)
    r = x - hi.astype(F32)
    mid = r.astype(BF16)
    lo = (r - mid.astype(F32)).astype(BF16)
    return hi, mid, lo


def _dot3_left(w, x):
    hi, mid, lo = _split3(x)
    return _dot(w, hi) + _dot(w, mid) + _dot(w, lo)


def _dot3_right(x, w):
    hi, mid, lo = _split3(x)
    return _dot(hi, w) + _dot(mid, w) + _dot(lo, w)


def _log_sigmoid(x):
    return jnp.minimum(x, 0.0) - jnp.log1p(jnp.exp(-jnp.abs(x)))


def _sigmoid(x):
    return 1.0 / (1.0 + jnp.exp(-x))


def _params(sem):
    return pltpu.CompilerParams(dimension_semantics=sem, vmem_limit_bytes=VMEM_LIMIT)


def _const_spec(shape):
    nd = len(shape)
    return pl.BlockSpec(shape, lambda *_: (0,) * nd)


def _inproj_kernel(x_ref, g1_ref, wp_ref, bf_ref, gq_ref, gk_ref, g512_ref, wa2_ref, ba_ref,
                   cos_ref, sin_ref, tri_ref, place_ref, placec_ref, qone_ref,
                   qa_ref, k32_ref, v32_ref, kbf_ref, vbf_ref, logf_ref, c_ref,
                   qb_ref, kb_ref, vb_ref, la_ref, gb_ref,
                   qc_ref, kc_ref, vc_ref, gc_ref,
                   qaug_ref, kaug_ref, vt_ref, kt32_ref, vt32_ref,
                   carry_ref, *, blocks_per_seq):
    i = pl.program_id(0)
    x = x_ref[...]
    ms = jnp.mean(x * x, axis=-1, keepdims=True)
    xn = (x * lax.rsqrt(ms + NORM_EPS) * g1_ref[...]).astype(BF16)

    za = _dot(xn, wp_ref[:, _OFF_QA:_OFF_QB])
    g512 = g512_ref[...]

    def head_rms(t, gain):
        msq = _dot((t * t).astype(BF16), g512)
        return t * lax.rsqrt(msq + NORM_EPS) * gain

    qn = head_rms(za[:, 0:512], gq_ref[...])
    qa_ref[...] = qn.astype(BF16)
    k = head_rms(za[:, 512:1024], gk_ref[...])
    k32_ref[...] = k
    kbf = k.astype(BF16)
    kbf_ref[...] = kbf
    v = za[:, 1024:1536]
    v32_ref[...] = v
    vbf_ref[...] = v.astype(BF16)
    v_t = v.T
    vt_ref[...] = v_t.astype(BF16)
    vt32_ref[0] = v_t
    kt32_ref[0] = k.T
    logf = _log_sigmoid(za[:, 1536:1664] + bf_ref[...])
    logf_ref[...] = logf

    @pl.when(i % blocks_per_seq == 0)
    def _():
        carry_ref[...] = jnp.zeros_like(carry_ref)

    c = _dot3_left(tri_ref[...], logf) + carry_ref[...]
    c_ref[...] = c
    carry_ref[...] = c[c.shape[0] - 1:, :]

    place = place_ref[...]
    qaug_ref[...] = (_dot((qn * LOG2E).astype(BF16), place) + qone_ref[...]).astype(BF16)
    chi, cmid, clo = _split3(c * (-LOG2E))
    kaug = (_dot(kbf, place) + _dot(chi, placec_ref[0]) + _dot(cmid, placec_ref[1]) + _dot(clo, placec_ref[2]))
    kaug_ref[...] = kaug.astype(BF16)

    zb = _dot(xn, wp_ref[:, _OFF_QB:_OFF_QC])
    qb_ref[...] = zb[:, 0:128]
    kb_ref[...] = zb[:, 128:256]
    vb_ref[...] = zb[:, 256:512].astype(BF16)
    ab = zb[:, 512:640].astype(BF16)
    la_ref[...] = _log_sigmoid(_dot(ab, wa2_ref[...]) + ba_ref[...]) * (1.0 / GLA_TAU)
    gb_ref[...] = zb[:, 640:896]

    zc = _dot(xn, wp_ref[:, _OFF_QC:D_IN_PAD])
    cos = cos_ref[...]
    sin = sin_ref[...]

    def rot(t):
        t1, t2 = t[:, 0:128], t[:, 128:256]
        return jnp.concatenate([t1 * cos - t2 * sin, t1 * sin + t2 * cos], axis=1)

    qc_ref[...] = rot(zc[:, 0:256])
    kc_ref[...] = rot(zc[:, 256:512]) * (DK_C ** -0.5)
    vc_ref[...] = zc[:, 512:768].astype(BF16)
    gc_ref[...] = zc[:, 768:1024]


def _inproj(x2d, lw, cos_tab, sin_tab, tm, blocks_per_seq):
    n = x2d.shape[0]
    nb_tab = cos_tab.shape[0] // tm
    row = lambda w: pl.BlockSpec((tm, w), lambda i: (i, 0))
    tab = pl.BlockSpec((tm, LANES), lambda i: (i % nb_tab, 0))
    tri = jnp.tril(jnp.ones((tm, tm), F32)).astype(BF16)
    widths = (512, 512, 512, 512, 512, 128, 128, 128, 128, 256, 128, 256, 256, 256, 256, 256, AUG_W, AUG_W)
    dtypes = (BF16, F32, F32, BF16, BF16, F32, F32, F32, F32, BF16, F32, F32, F32, F32, BF16, F32, BF16, BF16)
    consts = (lw["g1"], lw["wp"], lw["bf"], lw["gq"], lw["gk"], lw["g512"], lw["wa2"], lw["ba"])
    place_np = np.zeros((W_A, AUG_W), np.float32)
    place_np[np.arange(W_A), (np.arange(W_A) // HEAD_DIM) * LANES + np.arange(W_A) % HEAD_DIM] = 1.0
    placec_np = np.zeros((3, LANES, AUG_W), np.float32)
    qone_np = np.zeros((1, AUG_W), np.float32)
    for piece in range(3):
        placec_np[piece, np.arange(H_A), np.arange(H_A) * LANES + HEAD_DIM + piece] = 1.0
        qone_np[0, np.arange(H_A) * LANES + HEAD_DIM + piece] = 1.0
    extra = (jnp.asarray(place_np, BF16), jnp.asarray(placec_np, BF16), jnp.asarray(qone_np))
    return pl.pallas_call(
        functools.partial(_inproj_kernel, blocks_per_seq=blocks_per_seq),
        grid=(n // tm,),
        in_specs=([row(D_MODEL)] + [_const_spec(a.shape) for a in consts] + [tab, tab, _const_spec(tri.shape)]
                  + [_const_spec(a.shape) for a in extra]),
        out_specs=([row(w) for w in widths] + [pl.BlockSpec((W_A, tm), lambda i: (0, i))]
                   + [pl.BlockSpec((1, W_A, tm), lambda i: (i // blocks_per_seq, 0, i % blocks_per_seq))] * 2),
        out_shape=([jax.ShapeDtypeStruct((n, w), d) for w, d in zip(widths, dtypes)]
                   + [jax.ShapeDtypeStruct((W_A, n), BF16)]
                   + [jax.ShapeDtypeStruct((n // (blocks_per_seq * tm), W_A, blocks_per_seq * tm), F32)] * 2),
        scratch_shapes=[pltpu.VMEM((1, LANES), F32)],
        compiler_params=_params(("arbitrary",)),
        name="inproj",
    )(x2d, *consts, cos_tab, sin_tab, tri, *extra)


def _flash_kernel(qi_ref, ki_ref, q_ref, k_ref, vt_ref, o_ref, m_ref, l_ref, acc_ref, *, heads):
    step = pl.program_id(2)
    qi = qi_ref[step]
    ki = ki_ref[step]
    tq = q_ref.shape[1]
    tk = k_ref.shape[1]

    @pl.when(ki == 0)
    def _():
        m_ref[...] = jnp.full_like(m_ref, NEG_BIG)
        l_ref[...] = jnp.zeros_like(l_ref)
        acc_ref[...] = jnp.zeros_like(acc_ref)

    def body(diagonal):
        if diagonal:
            keep = (lax.broadcasted_iota(jnp.int32, (tk, tq), 0)
                    <= lax.broadcasted_iota(jnp.int32, (tk, tq), 1))
        for hh in range(heads):
            qh = q_ref[0, :, hh * LANES:(hh + 1) * LANES]
            kh = k_ref[0, :, hh * LANES:(hh + 1) * LANES]
            st = _dot_nt(kh, qh)
            if diagonal:
                st = jnp.where(keep, st, NEG_BIG)
            m_prev = m_ref[hh]
            m_new = jnp.maximum(m_prev, jnp.max(st, axis=0, keepdims=True))
            alpha = jnp.exp2(m_prev - m_new)
            p = jnp.exp2(st - m_new)
            l_ref[hh] = alpha * l_ref[hh] + jnp.sum(p, axis=0, keepdims=True)
            vth = vt_ref[hh * HEAD_DIM:(hh + 1) * HEAD_DIM, :]
            acc_ref[hh] = alpha * acc_ref[hh] + _dot(vth, p.astype(BF16))
            m_ref[hh] = m_new

    @pl.when(ki < qi)
    def _():
        body(False)

    @pl.when(ki == qi)
    def _():
        body(True)
        ot = jnp.concatenate([acc_ref[hh] / l_ref[hh] for hh in range(heads)], axis=0)
        o_ref[0] = ot.T.astype(o_ref.dtype)


def _flash(q_aug, k_aug, v_t, t, heads):
    b, s, _ = q_aug.shape
    nq = s // t
    qi_np = np.array([i for i in range(nq) for _ in range(i + 1)], np.int32)
    ki_np = np.array([j for i in range(nq) for j in range(i + 1)], np.int32)
    grid_spec = pltpu.PrefetchScalarGridSpec(
        num_scalar_prefetch=2,
        grid=(b, H_A // heads, len(qi_np)),
        in_specs=[
            pl.BlockSpec((1, t, heads * LANES), lambda bb, g, st, qi, ki: (bb, qi[st], g)),
            pl.BlockSpec((1, t, heads * LANES), lambda bb, g, st, qi, ki: (bb, ki[st], g)),
            pl.BlockSpec((heads * HEAD_DIM, t), lambda bb, g, st, qi, ki: (g, bb * nq + ki[st])),
        ],
        out_specs=pl.BlockSpec((1, t, heads * HEAD_DIM), lambda bb, g, st, qi, ki: (bb, qi[st], g)),
        scratch_shapes=[pltpu.VMEM((heads, 1, t), F32), pltpu.VMEM((heads, 1, t), F32),
                        pltpu.VMEM((heads, HEAD_DIM, t), F32)],
    )
    return pl.pallas_call(
        functools.partial(_flash_kernel, heads=heads),
        grid_spec=grid_spec,
        out_shape=jax.ShapeDtypeStruct((b, s, W_A), BF16),
        compiler_params=_params(("arbitrary", "arbitrary", "arbitrary")),
        name="fox_prompt",
    )(jnp.asarray(qi_np), jnp.asarray(ki_np), q_aug, k_aug, v_t)


def _decode_kernel(pt_ref, qm_ref, kn_ref, vn_ref, lfn_ref, w_ref, *rest, n_pages):
    g = n_pages
    k_refs = rest[0:g]
    v_refs = rest[g:2 * g]
    lf_refs = rest[2 * g:3 * g]
    o_ref, m_ref, l_ref, acc_ref, carry_ref = rest[3 * g:]
    j = pl.program_id(1)
    qm = qm_ref[0]
    rows = qm.shape[0]

    def attend(bias, kt, vt):
        s = _dot(qm, kt) + bias
        m_prev = m_ref[...]
        m_new = jnp.maximum(m_prev, jnp.max(s, axis=1, keepdims=True))
        alpha = jnp.exp(m_prev - m_new)
        p = jnp.exp(s - m_new)
        l_ref[...] = alpha * l_ref[...] + jnp.sum(p, axis=1, keepdims=True)
        acc_ref[...] = alpha * acc_ref[...] + _dot_nt(p.astype(BF16), vt)
        m_ref[...] = m_new

    w = w_ref[...]

    @pl.when(j == 0)
    def _():
        m_ref[...] = jnp.full_like(m_ref, NEG_BIG)
        l_ref[...] = jnp.zeros_like(l_ref)
        acc_ref[...] = jnp.zeros_like(acc_ref)
        carry_ref[...] = jnp.zeros_like(carry_ref)
        y = _dot3_right(lfn_ref[0], w)
        cum = y[:, LANES:] - y[:, :LANES]
        bias = jnp.concatenate([-cum] * (rows // H_A), axis=0)
        tok = lax.broadcasted_iota(jnp.int32, (rows, LANES), 0) // H_A
        col = lax.broadcasted_iota(jnp.int32, (rows, LANES), 1)
        attend(jnp.where(col <= tok, bias, NEG_BIG), kn_ref[0], vn_ref[0])

    x = jnp.concatenate([r[...] for r in lf_refs], axis=0)
    y = _dot3_right(x, w)
    off = carry_ref[...]
    pieces = []
    for i in range(g):
        pieces.append(y[H_A * i:H_A * (i + 1), :LANES] + off)
        off = off + y[H_A * i:H_A * (i + 1), LANES:]
    carry_ref[...] = off
    bias_t = jnp.concatenate(pieces, axis=1)
    bias = jnp.concatenate([bias_t] * (rows // H_A), axis=0)
    page_t = lambda r: r[...].reshape(W_A, PAGE_SIZE).astype(BF16)
    kt = jnp.concatenate([page_t(r) for r in k_refs], axis=1)
    vt = jnp.concatenate([page_t(r) for r in v_refs], axis=1)
    attend(bias, kt, vt)

    @pl.when(j == pl.num_programs(1) - 1)
    def _():
        o = acc_ref[...] / l_ref[...]
        head_row = lax.broadcasted_iota(jnp.int32, o.shape, 0) % H_A
        head_col = lax.broadcasted_iota(jnp.int32, o.shape, 1) // HEAD_DIM
        o = jnp.where(head_row == head_col, o, 0.0)
        t = rows // H_A
        o_ref[0] = jnp.sum(o.reshape(t, H_A, o.shape[1]), axis=1).astype(o_ref.dtype)


def _decode(layer, qm, k_new, v_new, lf_new_t, cache_k, cache_v, cache_lf_t, page_table, g):
    db, n_pages = page_table.shape
    t = qm.shape[1] // H_A
    steps = n_pages // g
    suffix = (np.arange(LANES)[:, None] > np.arange(LANES)[None, :]).astype(np.float32)
    w = jnp.asarray(np.concatenate([suffix, np.ones((LANES, LANES), np.float32)], axis=1), BF16)

    def page_map(i, trailing):
        def index_map(b, j, pt):
            return (layer, pt[b, n_pages - 1 - (j * g + i)]) + (0,) * trailing
        return index_map

    seq = lambda shape: pl.BlockSpec((1,) + shape, lambda b, j, pt: (b, 0, 0))
    in_specs = [seq(qm.shape[1:]), seq((W_A, LANES)), seq((W_A, LANES)), seq((H_A, LANES)),
                pl.BlockSpec(w.shape, lambda b, j, pt: (0, 0))]
    kv_page = (None, None, H_A, HEAD_DIM, PAGE_SIZE)
    in_specs += [pl.BlockSpec(kv_page, page_map(i, 3)) for i in range(g)]
    in_specs += [pl.BlockSpec(kv_page, page_map(i, 3)) for i in range(g)]
    in_specs += [pl.BlockSpec((None, None, H_A, PAGE_SIZE), page_map(i, 2)) for i in range(g)]
    rows = qm.shape[1]
    grid_spec = pltpu.PrefetchScalarGridSpec(
        num_scalar_prefetch=1,
        grid=(db, steps),
        in_specs=in_specs,
        out_specs=pl.BlockSpec((1, t, W_A), lambda b, j, pt: (b, 0, 0)),
        scratch_shapes=[pltpu.VMEM((rows, 1), F32), pltpu.VMEM((rows, 1), F32),
                        pltpu.VMEM((rows, W_A), F32), pltpu.VMEM((H_A, LANES), F32)],
    )
    return pl.pallas_call(
        functools.partial(_decode_kernel, n_pages=g),
        grid_spec=grid_spec,
        out_shape=jax.ShapeDtypeStruct((db, t, W_A), BF16),
        compiler_params=_params(("arbitrary", "arbitrary")),
        name="fox_sample",
    )(page_table, qm, k_new, v_new, lf_new_t, w,
      *([cache_k] * g), *([cache_v] * g), *([cache_lf_t] * g))


def _recurrent_kernel(qb_ref, kb_ref, vb_ref, la_ref, gb_ref, qc_ref, kc_ref, vc_ref, gc_ref,
                      sg0_ref, sr0_ref, tri_ref, g256_ref, gno_ref, rno_ref,
                      dq_ref, kdec_ref, dmat_ref, sdec_ref, mg_ref, mr_ref,
                      ob_ref, oc_ref, sg_ref, sr_ref, stg_ref, str_ref):
    j = pl.program_id(1)
    c = tri_ref.shape[0]

    @pl.when(j == 0)
    def _():
        stg_ref[...] = sg0_ref[0]
        str_ref[...] = sr0_ref[0]

    tri = tri_ref[...]
    causal = (lax.broadcasted_iota(jnp.int32, (c, c), 1) <= lax.broadcasted_iota(jnp.int32, (c, c), 0))
    lane_k = lax.broadcasted_iota(jnp.int32, (1, LANES), 1)
    lane_v = lax.broadcasted_iota(jnp.int32, (1, 2 * LANES), 1)
    g256 = g256_ref[...]

    def chunk(rs):
        la = la_ref[rs, :]
        hi = la.astype(BF16)
        lo = (la - hi.astype(F32)).astype(BF16)
        b = _dot(tri, hi) + _dot(tri, lo)
        b_last = b[c - 1:, :]
        qs = (qb_ref[rs, :] * (jnp.exp(b) * (DK_B ** -0.5))).astype(BF16)
        kinv = (kb_ref[rs, :] * jnp.exp(-b)).astype(BF16)
        kend = (kb_ref[rs, :] * jnp.exp(b_last - b)).astype(BF16)
        vb = vb_ref[rs, :]
        st = stg_ref[...]
        o = _dot_nt(qs, st.astype(BF16))
        for h in range(H_B):
            qh = jnp.where(lane_k // DK_B == h, qs, jnp.zeros_like(qs))
            att = jnp.where(causal, _dot_nt(qh, kinv), 0.0)
            vh = jnp.where(lane_v // DV_B == h, vb, jnp.zeros_like(vb))
            o = o + _dot(att.astype(BF16), vh)
        stg_ref[...] = st * jnp.exp(b_last) + _dot_tn(vb, kend) * mg_ref[...]
        msq = _dot((o * o).astype(BF16), g256)
        gate = gb_ref[rs, :]
        ob_ref[rs, :] = (o * lax.rsqrt(msq + NORM_EPS) * gno_ref[...] * (gate * _sigmoid(gate))).astype(ob_ref.dtype)

        q = qc_ref[rs, :].astype(BF16)
        kc = kc_ref[rs, :]
        vc = vc_ref[rs, :]
        st = str_ref[...]
        o = _dot_nt(q, st.astype(BF16)) * dq_ref[...]
        kbf = kc.astype(BF16)
        for h in range(H_C):
            qh = jnp.where((lane_v % LANES) // (DK_C // 2) == h, q, jnp.zeros_like(q))
            att = _dot_nt(qh, kbf) * dmat_ref[h]
            vh = jnp.where(lane_v // DV_C == h, vc, jnp.zeros_like(vc))
            o = o + _dot(att.astype(BF16), vh)
        str_ref[...] = st * sdec_ref[...] + _dot_tn(vc, (kc * kdec_ref[...]).astype(BF16)) * mr_ref[...]
        ohi = o.astype(BF16)
        olo = (o - ohi.astype(F32)).astype(BF16)
        mu = _dot(ohi, g256) + _dot(olo, g256)
        d = o - mu
        var = _dot((d * d).astype(BF16), g256)
        gate = gc_ref[rs, :]
        oc_ref[rs, :] = (d * lax.rsqrt(var + NORM_EPS) * rno_ref[...] * (gate * _sigmoid(gate))).astype(oc_ref.dtype)

    for sub in range(qb_ref.shape[0] // c):
        chunk(slice(sub * c, (sub + 1) * c))

    @pl.when(j == pl.num_programs(1) - 1)
    def _():
        sg_ref[0] = stg_ref[...]
        sr_ref[0] = str_ref[...]


def _recurrent(rec_in, sg0, sr0, lw, b, t, c):
    rows = c * min(REC_CHUNKS_PER_STEP, t // c)
    nchunk = t // rows
    tri = jnp.tril(jnp.ones((c, c), F32)).astype(BF16)
    lg = jnp.log1p(-jnp.exp2(-5.0 - jnp.arange(H_C, dtype=F32)))
    pos = jnp.arange(c, dtype=F32)
    lg_v = jnp.repeat(lg, DV_C)
    lg_k = jnp.tile(jnp.repeat(lg, DK_C // 2), 2)
    dq = jnp.exp((pos[:, None] + 1.0) * lg_v[None, :])
    kdec = jnp.exp((c - 1.0 - pos)[:, None] * lg_k[None, :])
    diff = pos[:, None] - pos[None, :]
    dmat = jnp.where(diff >= 0, jnp.exp(jnp.maximum(diff, 0.0)[None] * lg[:, None, None]), 0.0)
    sdec = jnp.exp(c * lg_k)[None, :]
    hv = np.arange(W_B) // DV_B
    mg = jnp.asarray((hv[:, None] == (np.arange(H_B * DK_B) // DK_B)[None, :]).astype(np.float32))
    mr = jnp.asarray((hv[:, None] == ((np.arange(H_C * DK_C) % LANES) // (DK_C // 2))[None, :]).astype(np.float32))
    widths = (128, 128, 256, 128, 256, 256, 256, 256, 256)
    row = lambda w: pl.BlockSpec((rows, w), lambda bb, j: (bb * nchunk + j, 0))
    per_b = lambda shape: pl.BlockSpec((1,) + shape, lambda bb, j: (bb, 0, 0))
    const = lambda a: pl.BlockSpec(a.shape, lambda bb, j: (0,) * a.ndim)
    consts = (tri, lw["g256"], lw["gno"], lw["rno"], dq, kdec, dmat, sdec, mg, mr)
    n = b * t
    return pl.pallas_call(
        _recurrent_kernel,
        grid=(b, nchunk),
        in_specs=[row(w) for w in widths] + [per_b(sg0.shape[1:]), per_b(sr0.shape[1:])] + [const(a) for a in consts],
        out_specs=[row(256), row(256), per_b(sg0.shape[1:]), per_b(sr0.shape[1:])],
        out_shape=[jax.ShapeDtypeStruct((n, 256), BF16), jax.ShapeDtypeStruct((n, 256), BF16),
                   jax.ShapeDtypeStruct(sg0.shape, F32), jax.ShapeDtypeStruct(sr0.shape, F32)],
        scratch_shapes=[pltpu.VMEM(sg0.shape[1:], F32), pltpu.VMEM(sr0.shape[1:], F32)],
        compiler_params=_params(("arbitrary", "arbitrary")),
        name="recurrent",
    )(*rec_in, sg0, sr0, *consts)


def _post_kernel(h_ref, oa_ref, ob_ref, oc_ref, p_ref, wo_ref, n2_ref, wu_ref, wd_ref, pn_ref,
                 wpg_ref, wpe_ref, out_ref):
    mix = jnp.concatenate([oa_ref[...], ob_ref[...], oc_ref[...]], axis=1)
    out_ref[...] = h_ref[...] + _dot(mix, wo_ref[...])

    def rms(x, g):
        return (x * lax.rsqrt(jnp.mean(x * x, axis=-1, keepdims=True) + NORM_EPS) * g).astype(BF16)

    xn = rms(out_ref[...], n2_ref[...])
    step = D_FF // 4
    for ci in range(4):
        u = jnp.maximum(_dot(xn, wu_ref[:, ci * step:(ci + 1) * step]), 0.0)
        out_ref[...] += _dot((u * u).astype(BF16), wd_ref[ci * step:(ci + 1) * step, :])
    h2 = out_ref[...]
    xn = rms(h2, pn_ref[...])
    gate = _sigmoid(_dot(xn, wpg_ref[...]))
    out_ref[...] = h2 + gate * _dot(p_ref[...].astype(BF16), wpe_ref[...])


def _post(h2d, oa, ob, oc, p2d, lw, tm):
    n = h2d.shape[0]
    row = lambda w: pl.BlockSpec((tm, w), lambda i: (i, 0))
    consts = (lw["wo"], lw["n2"], lw["wu"], lw["wd"], lw["pn"], lw["wpg"], lw["wpe"])
    const = lambda a: pl.BlockSpec(memory_space=pltpu.VMEM)
    return pl.pallas_call(
        _post_kernel,
        grid=(n // tm,),
        in_specs=[row(D_MODEL), row(W_A), row(W_B), row(W_C), row(D_PLE)] + [const(a) for a in consts],
        out_specs=row(D_MODEL),
        out_shape=jax.ShapeDtypeStruct((n, D_MODEL), F32),
        compiler_params=_params(("arbitrary",)),
        name="post",
    )(h2d, oa, ob, oc, p2d, *consts)


def _ret_perm():
    half = DK_C // 2
    return np.array([h * DK_C + part * half + j for part in range(2) for h in range(H_C) for j in range(half)])


def _layer_weights(l, norm1, w_in, b_f, fox_q_norm, fox_k_norm, gla_w_a2, gla_b_a, gla_o_norm,
                   ret_o_norm, w_out, norm2, w_up, w_down, ple_norm, w_pg, w_pe):
    (qa, ka, va, fa, qb, kb, vb, ab, gb, qc, kc, vc, gc) = jnp.split(w_in[l], SPLIT_IDX, axis=1)
    perm = _ret_perm()
    pad = lambda a: jnp.pad(a, ((0, 0), (0, LANES - a.shape[1])))
    wp = jnp.concatenate([qa, ka, va, pad(fa), qb, kb, vb, pad(ab), gb,
                          qc[:, perm], kc[:, perm], vc, gc], axis=1).astype(BF16)
    grp = np.arange(W_A) // HEAD_DIM
    g512 = jnp.asarray((grp[:, None] == grp[None, :]).astype(np.float32) / HEAD_DIM, BF16)
    row = lambda a: a.reshape(1, -1).astype(F32)
    return dict(
        g1=row(norm1[l]), wp=wp,
        bf=jnp.pad(row(b_f[l]), ((0, 0), (0, LANES - H_A))),
        gq=row(jnp.tile(fox_q_norm[l], H_A)) * (HEAD_DIM ** -0.5),
        gk=row(jnp.tile(fox_k_norm[l], H_A)),
        g512=g512, g256=g512[:W_B, :W_B],
        wa2=jnp.pad(gla_w_a2[l], ((0, LANES - GLA_RANK), (0, 0))).astype(BF16),
        ba=row(gla_b_a[l]),
        gno=row(gla_o_norm[l]), rno=row(ret_o_norm[l]),
        wo=w_out[l].astype(BF16), n2=row(norm2[l]), wu=w_up[l].astype(BF16), wd=w_down[l].astype(BF16),
        pn=row(ple_norm[l]), wpg=w_pg[l].astype(BF16), wpe=w_pe[l].astype(BF16),
    )


def _rotary_tables(pos):
    half = DK_C // 2
    freqs = ROPE_BASE ** (-jnp.arange(half, dtype=F32) / half)
    ang = pos[:, None] * freqs[None, :]
    return jnp.tile(jnp.cos(ang), (1, H_C)), jnp.tile(jnp.sin(ang), (1, H_C))


def _embed_states(s_gla, s_ret):
    b = s_gla.shape[0]
    eye_b = jnp.eye(H_B, dtype=F32)
    sg = jnp.einsum("bhkv,hg->bhvgk", s_gla.astype(F32), eye_b).reshape(b, W_B, H_B * DK_B)
    eye_c = jnp.eye(H_C, dtype=F32)
    half = DK_C // 2
    sr = s_ret.astype(F32).reshape(b, H_C, 2, half, DV_C)
    sr = jnp.einsum("bhpjv,hg->bhvpgj", sr, eye_c).reshape(b, W_C, H_C * DK_C)
    return sg, sr


def _extract_states(sg, sr):
    b = sg.shape[0]
    sg = sg.reshape(b, H_B, DV_B, H_B, DK_B)
    s_gla = jnp.stack([sg[:, h, :, h, :] for h in range(H_B)], axis=1)
    half = DK_C // 2
    sr = sr.reshape(b, H_C, DV_C, 2, H_C, half)
    s_ret = jnp.stack([sr[:, h, :, :, h, :] for h in range(H_C)], axis=1)
    s_ret = s_ret.reshape(b, H_C, DV_C, DK_C)
    return jnp.swapaxes(s_gla, 2, 3), jnp.swapaxes(s_ret, 2, 3)


def _decode_queries(qa):
    b, t, w = qa.shape
    head_col = np.arange(w) // HEAD_DIM
    mask = jnp.asarray(np.arange(H_A)[:, None] == head_col[None, :])
    return jnp.where(mask[None, None], qa[:, :, None, :], jnp.zeros((), qa.dtype)).reshape(b, t * H_A, w)


def kernel(x_prompt, x_sample, cache_k, cache_v, cache_logf, state_gla, state_ret, page_table,
           p_prompt, p_sample, norm1, w_in, b_f, fox_q_norm, fox_k_norm, gla_w_a2, gla_b_a,
           gla_o_norm, ret_o_norm, w_out, norm2, w_up, w_down, ple_norm, w_pg, w_pe):
    bp, sp, _ = x_prompt.shape
    db, ts, _ = x_sample.shape
    depth = w_in.shape[0]
    n_pool = cache_k.shape[1]
    past = page_table.shape[1] * PAGE_SIZE
    np_rows, ns_rows = bp * sp, db * ts
    tm_p = min(ROW_TILE, sp)
    tm_s = ns_rows
    att_t = min(ATT_TILE, sp)
    chunk_p = int(np.gcd(sp, GLA_CHUNK))
    chunk_s = int(np.gcd(ts, GLA_CHUNK))
    g_pages = min(PAGES_PER_STEP, page_table.shape[1])

    cos_p, sin_p = _rotary_tables(jnp.arange(sp, dtype=F32))
    cos_s, sin_s = _rotary_tables(past + jnp.arange(ts, dtype=F32))
    cos_s, sin_s = jnp.tile(cos_s, (db, 1)), jnp.tile(sin_s, (db, 1))

    ck = jnp.transpose(cache_k, (0, 1, 3, 4, 2))
    cv = jnp.transpose(cache_v, (0, 1, 3, 4, 2))
    clf_t = jnp.swapaxes(cache_logf, 2, 3)

    zero_g, zero_r = _embed_states(jnp.zeros((bp, H_B, DK_B, DV_B), F32), jnp.zeros((bp, H_C, DK_C, DV_C), F32))

    hp = x_prompt.reshape(np_rows, D_MODEL)
    hs = x_sample.reshape(ns_rows, D_MODEL)
    outs_p, outs_s = [], []
    for l in range(depth):
        lw = _layer_weights(l, norm1, w_in, b_f, fox_q_norm, fox_k_norm, gla_w_a2, gla_b_a, gla_o_norm,
                            ret_o_norm, w_out, norm2, w_up, w_down, ple_norm, w_pg, w_pe)

        (qa, k32, v32, kbf, vbf, logf, csum, *rec_in, q_aug, k_aug, v_t, kt32, vt32) = _inproj(
            hp, lw, cos_p, sin_p, tm_p, sp // tm_p)
        oa = _flash(q_aug.reshape(bp, sp, AUG_W), k_aug.reshape(bp, sp, AUG_W), v_t, att_t, ATT_HEADS)
        ob, oc, sg, sr = _recurrent(rec_in, zero_g, zero_r, lw, bp, sp, chunk_p)
        hp = _post(hp, oa.reshape(np_rows, W_A), ob, oc, p_prompt[l].reshape(np_rows, D_PLE), lw, tm_p)
        s_gla, s_ret = _extract_states(sg, sr)
        to_cache = lambda a: jnp.transpose(a.reshape(bp, H_A, HEAD_DIM, sp), (0, 3, 1, 2))
        outs_p.append((to_cache(kt32), to_cache(vt32), logf[:, :H_A].reshape(bp, sp, H_A), s_gla, s_ret))

        (qa, k32, v32, kbf, vbf, logf, csum, *rec_in, _, _, _, _, _) = _inproj(hs, lw, cos_s, sin_s, tm_s, 1)
        pad_rows = lambda a: jnp.pad(jnp.swapaxes(a.reshape(db, ts, W_A), 1, 2), ((0, 0), (0, 0), (0, LANES - ts)))
        lf_new_t = jnp.pad(jnp.swapaxes(logf[:, :H_A].reshape(db, ts, H_A), 1, 2), ((0, 0), (0, 0), (0, LANES - ts)))
        oa = _decode(l, _decode_queries(qa.reshape(db, ts, W_A)), pad_rows(kbf), pad_rows(vbf), lf_new_t,
                     ck, cv, clf_t, page_table, g_pages)
        sg0, sr0 = _embed_states(state_gla[l], state_ret[l])
        ob, oc, sg, sr = _recurrent(rec_in, sg0, sr0, lw, db, ts, chunk_s)
        hs = _post(hs, oa.reshape(ns_rows, W_A), ob, oc, p_sample[l].reshape(ns_rows, D_PLE), lw, tm_s)
        s_gla, s_ret = _extract_states(sg, sr)
        outs_s.append((k32.reshape(db, ts, H_A, HEAD_DIM), v32.reshape(db, ts, H_A, HEAD_DIM),
                       logf[:, :H_A].reshape(db, ts, H_A), s_gla, s_ret))

    k_p, v_p, lf_p, gla_p, ret_p = [jnp.stack(a) for a in zip(*outs_p)]
    k_s, v_s, lf_s, gla_s, ret_s = [jnp.stack(a) for a in zip(*outs_s)]
    return (hp.reshape(bp, sp, D_MODEL), hs.reshape(db, ts, D_MODEL),
            k_p, v_p, lf_p, gla_p, ret_p, k_s, v_s, lf_s, gla_s, ret_s)
```
